```python
import math
import jax, jax.numpy as jnp
from jax import lax
import numpy as np

D_MODEL = 4096
BATCH = 2
SEQ = 4096
DEPTH = 4

CHUNK = 64
Q_BLOCK = 128
HEAD_DIM = 128
N_MIX_HEADS = D_MODEL // HEAD_DIM
MLA_HEADS = N_MIX_HEADS // 2
FOX_HEADS = N_MIX_HEADS // 4
SB_HEADS = N_MIX_HEADS // 4
MLA_Q_LORA = D_MODEL // 4
MLA_KV_LORA = D_MODEL // 8
MLA_NOPE = HEAD_DIM
MLA_ROPE = 64
MLA_V = HEAD_DIM
ROPE_THETA = 10000.0
FOX_WIDTH = FOX_HEADS * HEAD_DIM
SB_WIDTH = SB_HEADS * HEAD_DIM
N_BRANCH = 3
IN_SPLITS = (MLA_Q_LORA, MLA_KV_LORA, MLA_ROPE,
             FOX_WIDTH, FOX_WIDTH, FOX_WIDTH, FOX_HEADS,
             SB_WIDTH, SB_WIDTH, SB_WIDTH,
             N_BRANCH * D_MODEL)
IN_COLS = sum(IN_SPLITS)
D_FF = 2 * D_MODEL
CONV_WIDTH = 3
EPS = 1e-6
FORGET_BIAS_INIT = 3.0

kernel_name = 'hybrid_mla_fox_stickbreak_convffn'


def _rmsnorm(t, g):
    tf = t.astype(jnp.float32)
    y = tf * lax.rsqrt(jnp.mean(tf * tf, axis=-1, keepdims=True) + EPS)
    return (y * g.astype(jnp.float32)).astype(t.dtype)


def _split(t, sizes):
    idx = np.cumsum(np.array(sizes))[:-1].tolist()
    return jnp.split(t, idx, axis=-1)


def _rope_tables(seq, dtype):
    inv = 1.0 / (ROPE_THETA ** (jnp.arange(0, MLA_ROPE, 2, dtype=jnp.float32) / MLA_ROPE))
    ang = jnp.arange(seq, dtype=jnp.float32)[:, None] * inv[None, :]
    return jnp.cos(ang).astype(dtype), jnp.sin(ang).astype(dtype)


def _apply_rope(t, cos, sin):
    t1, t2 = jnp.split(t, 2, axis=-1)
    return jnp.concatenate([t1 * cos - t2 * sin, t1 * sin + t2 * cos], axis=-1)


def _to_blocks(t):
    b, s = t.shape[:2]
    t = t.reshape((b, s // Q_BLOCK, Q_BLOCK) + t.shape[2:])
    return jnp.moveaxis(t, 1, 0)


def _from_blocks(t):
    nb, b, qb = t.shape[:3]
    return jnp.moveaxis(t, 0, 1).reshape(b, nb * qb, -1)


def _mla_attention(q_nope, q_rope, k_nope, k_rope, v):
    seq = k_nope.shape[1]
    kpos = jnp.arange(seq)
    scale = 1.0 / math.sqrt(MLA_NOPE + MLA_ROPE)

    def block(args):
        qn, qr, i = args
        s = (jnp.einsum('bqhd,bkhd->bhqk', qn, k_nope)
             + jnp.einsum('bqhr,bkr->bhqk', qr, k_rope)).astype(jnp.float32) * scale
        qpos = i * Q_BLOCK + jnp.arange(Q_BLOCK)
        mask = (kpos[None, :] // CHUNK) <= (qpos[:, None] // CHUNK)
        p = jax.nn.softmax(jnp.where(mask, s, -jnp.inf), axis=-1)
        return jnp.einsum('bhqk,bkhd->bqhd', p.astype(v.dtype), v)

    nb = seq // Q_BLOCK
    out = lax.map(block, (_to_blocks(q_nope), _to_blocks(q_rope), jnp.arange(nb)))
    return _from_blocks(out)


def _fox_attention(q, k, v, log_f):
    seq = k.shape[1]
    kpos = jnp.arange(seq)
    scale = 1.0 / math.sqrt(HEAD_DIM)
    c = jnp.cumsum(log_f.astype(jnp.float32), axis=1)
    c_k = jnp.transpose(c, (0, 2, 1))[:, :, None, :]

    def block(args):
        qb, cq, i = args
        s = jnp.einsum('bqhd,bkhd->bhqk', qb, k).astype(jnp.float32) * scale
        s = s + jnp.transpose(cq, (0, 2, 1))[..., None] - c_k
        qpos = i * Q_BLOCK + jnp.arange(Q_BLOCK)
        mask = kpos[None, :] <= qpos[:, None]
        p = jax.nn.softmax(jnp.where(mask, s, -jnp.inf), axis=-1)
        return jnp.einsum('bhqk,bkhd->bqhd', p.astype(v.dtype), v)

    nb = seq // Q_BLOCK
    out = lax.map(block, (_to_blocks(q), _to_blocks(c), jnp.arange(nb)))
    return _from_blocks(out)


def _stick_breaking_attention(q, k, v):
    seq = k.shape[1]
    kpos = jnp.arange(seq)
    scale = 1.0 / math.sqrt(HEAD_DIM)

    def block(args):
        qb, i = args
        z = jnp.einsum('bqhd,bkhd->bhqk', qb, k).astype(jnp.float32) * scale
        qpos = i * Q_BLOCK + jnp.arange(Q_BLOCK)
        strict = kpos[None, :] < qpos[:, None]
        log_one_minus = jnp.where(strict, jax.nn.log_sigmoid(-z), 0.0)
        csum = jnp.cumsum(log_one_minus, axis=-1)
        rest = csum[..., -1:] - csum
        a = jnp.where(strict, jnp.exp(jax.nn.log_sigmoid(z) + rest), 0.0)
        return jnp.einsum('bhqk,bkhd->bqhd', a.astype(v.dtype), v)

    nb = seq // Q_BLOCK
    out = lax.map(block, (_to_blocks(q), jnp.arange(nb)))
    return _from_blocks(out)


def _causal_dwconv(t, w, b):
    seq = t.shape[1]
    tp = jnp.pad(t, ((0, 0), (CONV_WIDTH - 1, 0), (0, 0)))
    out = b
    for i in range(CONV_WIDTH):
        out = out + w[i] * tp[:, i:i + seq]
    return out


def setup_inputs(seed: int = 0) -> dict:
    key = jax.random.key(seed)
    ks = jax.random.split(key, 24)
    L = DEPTH

    def nrm(k, shape, scale):
        return jax.random.normal(k, shape, jnp.float32) * scale

    return {
        'x': nrm(ks[0], (BATCH, SEQ, D_MODEL), 1.0),
        'attn_norm': 1.0 + nrm(ks[1], (L, D_MODEL), 0.01),
        'w_in': nrm(ks[2], (L, D_MODEL, IN_COLS), D_MODEL ** -0.5),
        'b_forget': FORGET_BIAS_INIT + nrm(ks[3], (L, FOX_HEADS), 0.1),
        'b_gate': nrm(ks[4], (L, N_BRANCH * D_MODEL), 0.01),
        'q_norm': 1.0 + nrm(ks[5], (L, MLA_Q_LORA), 0.01),
        'w_uq': nrm(ks[6], (L, MLA_Q_LORA, MLA_HEADS * (MLA_NOPE + MLA_ROPE)), MLA_Q_LORA ** -0.5),
        'kv_norm': 1.0 + nrm(ks[7], (L, MLA_KV_LORA), 0.01),
        'w_ukv': nrm(ks[8], (L, MLA_KV_LORA, MLA_HEADS * (MLA_NOPE + MLA_V)), MLA_KV_LORA ** -0.5),
        'w_br_mla': nrm(ks[9], (L, MLA_HEADS * MLA_V, D_MODEL), (MLA_HEADS * MLA_V) ** -0.5),
        'w_br_fox': nrm(ks[10], (L, FOX_WIDTH, D_MODEL), FOX_WIDTH ** -0.5),
        'w_br_sb': nrm(ks[11], (L, SB_WIDTH, D_MODEL), SB_WIDTH ** -0.5),
        'w_o': nrm(ks[12], (L, D_MODEL, D_MODEL), D_MODEL ** -0.5),
        'ffn_norm': 1.0 + nrm(ks[13], (L, D_MODEL), 0.01),
        'w_ffn_gate': nrm(ks[14], (L, D_MODEL, D_FF), D_MODEL ** -0.5),
        'conv_w': nrm(ks[15], (L, CONV_WIDTH, D_FF), CONV_WIDTH ** -0.5),
        'conv_b': nrm(ks[16], (L, D_FF), 0.01),
        'w_ffn_up': nrm(ks[17], (L, D_MODEL, D_FF), D_MODEL ** -0.5),
        'w_ffn_down': nrm(ks[18], (L, D_FF, D_MODEL), D_FF ** -0.5),
        'final_norm': 1.0 + nrm(ks[19], (D_MODEL,), 0.01),
    }


def reference(x, attn_norm, w_in, b_forget, b_gate, q_norm, w_uq, kv_norm, w_ukv,
              w_br_mla, w_br_fox, w_br_sb, w_o, ffn_norm, w_ffn_gate, conv_w, conv_b,
              w_ffn_up, w_ffn_down, final_norm):
    b, s, _ = x.shape
    cos, sin = _rope_tables(s, x.dtype)
    for l in range(DEPTH):
        h = _rmsnorm(x, attn_norm[l])
        proj = jnp.einsum('bsd,dc->bsc', h, w_in[l])
        (q_lat, kv_lat, k_rope, fq, fk, fv, f_pre,
         sq, sk, sv, g_pre) = _split(proj, IN_SPLITS)

        q = jnp.einsum('bsr,rc->bsc', _rmsnorm(q_lat, q_norm[l]), w_uq[l])
        q = q.reshape(b, s, MLA_HEADS, MLA_NOPE + MLA_ROPE)
        q_nope, q_rope = q[..., :MLA_NOPE], q[..., MLA_NOPE:]
        q_rope = _apply_rope(q_rope, cos[None, :, None, :], sin[None, :, None, :])
        kv = jnp.einsum('bsr,rc->bsc', _rmsnorm(kv_lat, kv_norm[l]), w_ukv[l])
        kv = kv.reshape(b, s, MLA_HEADS, MLA_NOPE + MLA_V)
        k_nope, v_mla = kv[..., :MLA_NOPE], kv[..., MLA_NOPE:]
        k_rope = _apply_rope(k_rope, cos[None], sin[None])
        o_mla = _mla_attention(q_nope, q_rope, k_nope, k_rope, v_mla)

        hs = (b, s, FOX_HEADS, HEAD_DIM)
        log_f = jax.nn.log_sigmoid((f_pre + b_forget[l]).astype(jnp.float32))
        o_fox = _fox_attention(fq.reshape(hs), fk.reshape(hs), fv.reshape(hs), log_f)

        hs = (b, s, SB_HEADS, HEAD_DIM)
        o_sb = _stick_breaking_attention(sq.reshape(hs), sk.reshape(hs), sv.reshape(hs))

        g_a, g_b, g_c = jnp.split(jax.nn.sigmoid(g_pre + b_gate[l]), N_BRANCH, axis=-1)
        merged = (g_a * jnp.einsum('bsc,cd->bsd', o_mla, w_br_mla[l])
                  + g_b * jnp.einsum('bsc,cd->bsd', o_fox, w_br_fox[l])
                  + g_c * jnp.einsum('bsc,cd->bsd', o_sb, w_br_sb[l]))
        x = x + jnp.einsum('bsd,de->bse', merged, w_o[l])

        h = _rmsnorm(x, ffn_norm[l])
        gate = _causal_dwconv(jnp.einsum('bsd,df->bsf', h, w_ffn_gate[l]), conv_w[l], conv_b[l])
        up = jnp.einsum('bsd,df->bsf', h, w_ffn_up[l])
        x = x + jnp.einsum('bsf,fd->bsd', jax.nn.silu(gate) * up, w_ffn_down[l])
    return _rmsnorm(x, final_norm)
```

```python
import functools
import math

import jax
import jax.numpy as jnp
from jax import lax
from jax.experimental import pallas as pl
from jax.experimental.pallas import tpu as pltpu

HEAD_DIM = 128
CHUNK = 64
ROPE = 64
ROPE_THETA = 10000.0
CONV_WIDTH = 3
EPS = 1e-6
LANES = 128
NEG_BIG = -1e30
VMEM_LIMIT = 56 * 1024 * 1024

F32 = jnp.float32
BF16 = jnp.bfloat16


def _cparams(sem):
    return pltpu.CompilerParams(dimension_semantics=sem, vmem_limit_bytes=VMEM_LIMIT)


def _pick(n, pref):
    t = min(n, pref)
    while n % t:
        t //= 2
    return t


def _rms(x, g):
    return x * lax.rsqrt(jnp.mean(x * x, axis=-1, keepdims=True) + EPS) * g


def _sigmoid(x):
    return 1.0 / (1.0 + jnp.exp(-x))


def _log_sigmoid(x):
    return jnp.minimum(x, 0.0) - jnp.log(1.0 + jnp.exp(-jnp.abs(x)))


def _rmsnorm_kernel(x_ref, g_ref, o_ref):
    o_ref[...] = _rms(x_ref[...], g_ref[...]).astype(o_ref.dtype)


def rmsnorm(x, g, out_dtype, name):
    m, d = x.shape
    tm = _pick(m, 256)
    return pl.pallas_call(
        _rmsnorm_kernel,
        out_shape=jax.ShapeDtypeStruct((m, d), out_dtype),
        grid=(m // tm,),
        in_specs=[pl.BlockSpec((tm, d), lambda i: (i, 0)),
                  pl.BlockSpec((1, d), lambda i: (0, 0))],
        out_specs=pl.BlockSpec((tm, d), lambda i: (i, 0)),
        compiler_params=_cparams(("parallel",)),
        name=name,
    )(x, g.reshape(1, d))


def _mm_kernel(*refs, nk, n_extra, epilogue):
    x_ref, w_ref = refs[0], refs[1]
    extras = refs[2:2 + n_extra]
    o_ref = refs[2 + n_extra]
    if nk == 1:
        acc = jnp.dot(x_ref[...], w_ref[...], preferred_element_type=F32)
        o_ref[...] = epilogue(acc, *extras).astype(o_ref.dtype)
        return
    acc_ref = refs[3 + n_extra]
    k = pl.program_id(2)

    @pl.when(k == 0)
    def _():
        acc_ref[...] = jnp.dot(x_ref[...], w_ref[...], preferred_element_type=F32)

    @pl.when(k > 0)
    def _():
        acc_ref[...] += jnp.dot(x_ref[...], w_ref[...], preferred_element_type=F32)

    @pl.when(k == nk - 1)
    def _():
        o_ref[...] = epilogue(acc_ref[...], *extras).astype(o_ref.dtype)


def matmul(x, w, *, out_dtype, name, epilogue=None, extras=(), tm=1024, tn=512, tk=None):
    m, kd = x.shape
    _, n = w.shape
    tm, tn = _pick(m, tm), _pick(n, tn)
    tk = kd if tk is None else _pick(kd, tk)
    nk = kd // tk
    if epilogue is None:
        epilogue = lambda acc: acc
    in_specs = [pl.BlockSpec((tm, tk), lambda i, j, k: (i, k)),
                pl.BlockSpec((tk, tn), lambda i, j, k: (k, j))]
    args = [x, w]
    for arr, kind in extras:
        if kind == "row":
            in_specs.append(pl.BlockSpec((1, tn), lambda i, j, k: (0, j)))
        else:
            in_specs.append(pl.BlockSpec((tm, tn), lambda i, j, k: (i, j)))
        args.append(arr)
    scratch = [pltpu.VMEM((tm, tn), F32)] if nk > 1 else []
    return pl.pallas_call(
        functools.partial(_mm_kernel, nk=nk, n_extra=len(extras), epilogue=epilogue),
        out_shape=jax.ShapeDtypeStruct((m, n), out_dtype),
        grid=(m // tm, n // tn, nk),
        in_specs=in_specs,
        out_specs=pl.BlockSpec((tm, tn), lambda i, j, k: (i, j)),
        scratch_shapes=scratch,
        compiler_params=_cparams(("parallel", "parallel", "arbitrary")),
        name=name,
    )(*args)


def _ep_rmsnorm(acc, g_ref):
    return _rms(acc, g_ref[...])


def _ep_sigmoid_bias(acc, b_ref):
    return _sigmoid(acc + b_ref[...])


def _ep_residual(acc, r_ref):
    return r_ref[...] + acc


def _small_kernel(h_ref, w_ref, g_ref, bf_ref, cos_ref, sin_ref,
                  kvn_ref, kr_ref, lf_ref, *, kvl):
    acc = jnp.dot(h_ref[...], w_ref[...], preferred_element_type=F32)
    kvn_ref[...] = _rms(acc[:, :kvl], g_ref[...]).astype(kvn_ref.dtype)
    r = acc[:, kvl:kvl + LANES]
    p = acc[:, kvl + LANES:kvl + 2 * LANES]
    kr_ref[...] = (r * cos_ref[...] + p * sin_ref[...]).astype(kr_ref.dtype)
    lf_ref[...] = _log_sigmoid(acc[:, kvl + 2 * LANES:kvl + 3 * LANES] + bf_ref[...])


def inproj_small(h, w, kv_norm, b_forget_pad, cos_t, sin_t, seq, name):
    m, d = h.shape
    kvl = kv_norm.shape[0]
    n = w.shape[1]
    tm = _pick(seq, 512)
    nseq = seq // tm
    return pl.pallas_call(
        functools.partial(_small_kernel, kvl=kvl),
        out_shape=(jax.ShapeDtypeStruct((m, kvl), BF16),
                   jax.ShapeDtypeStruct((m, LANES), BF16),
                   jax.ShapeDtypeStruct((m, LANES), F32)),
        grid=(m // tm,),
        in_specs=[pl.BlockSpec((tm, d), lambda i: (i, 0)),
                  pl.BlockSpec((d, n), lambda i: (0, 0)),
                  pl.BlockSpec((1, kvl), lambda i: (0, 0)),
                  pl.BlockSpec((1, LANES), lambda i: (0, 0)),
                  pl.BlockSpec((tm, LANES), lambda i: (i % nseq, 0)),
                  pl.BlockSpec((tm, LANES), lambda i: (i % nseq, 0))],
        out_specs=(pl.BlockSpec((tm, kvl), lambda i: (i, 0)),
                   pl.BlockSpec((tm, LANES), lambda i: (i, 0)),
                   pl.BlockSpec((tm, LANES), lambda i: (i, 0))),
        compiler_params=_cparams(("parallel",)),
        name=name,
    )(h, w, kv_norm.reshape(1, kvl), b_forget_pad, cos_t, sin_t)


def _mlaq_kernel(x_ref, w_ref, cos_ref, sin_ref, o_ref, *, heads):
    hw = heads * HEAD_DIM
    x = x_ref[...]
    o_ref[:, :hw] = jnp.dot(x, w_ref[:, :hw], preferred_element_type=F32).astype(o_ref.dtype)
    r = jnp.dot(x, w_ref[:, hw:2 * hw], preferred_element_type=F32)
    p = jnp.dot(x, w_ref[:, 2 * hw:], preferred_element_type=F32)
    cos_t, sin_t = cos_ref[...], sin_ref[...]
    for h in range(heads):
        sl = slice(h * HEAD_DIM, (h + 1) * HEAD_DIM)
        o_ref[:, hw + h * HEAD_DIM:hw + (h + 1) * HEAD_DIM] = (
            r[:, sl] * cos_t + p[:, sl] * sin_t).astype(o_ref.dtype)


def mla_q(qn, w, cos_t, sin_t, seq, heads, name):
    m, ql = qn.shape
    n = w.shape[1]
    tm = _pick(seq, 256)
    nseq = seq // tm
    return pl.pallas_call(
        functools.partial(_mlaq_kernel, heads=heads),
        out_shape=jax.ShapeDtypeStruct((m, 2 * heads * HEAD_DIM), BF16),
        grid=(m // tm,),
        in_specs=[pl.BlockSpec((tm, ql), lambda i: (i, 0)),
                  pl.BlockSpec((ql, n), lambda i: (0, 0)),
                  pl.BlockSpec((tm, LANES), lambda i: (i % nseq, 0)),
                  pl.BlockSpec((tm, LANES), lambda i: (i % nseq, 0))],
        out_specs=pl.BlockSpec((tm, 2 * heads * HEAD_DIM), lambda i: (i, 0)),
        compiler_params=_cparams(("parallel",)),
        name=name,
    )(qn, w, cos_t, sin_t)


def _split3(x):
    hi = x.astype(BF16)
    r1 = x - hi.astype(F32)
    mid = r1.astype(BF16)
    lo = (r1 - mid.astype(F32)).astype(BF16)
    return hi, mid, lo


def _cumsum_kernel(x_ref, o_ref, *, seq, tc):
    rows = lax.broadcasted_iota(jnp.int32, (tc, tc), 0)
    cols = lax.broadcasted_iota(jnp.int32, (tc, tc), 1)
    tri = jnp.where(cols <= rows, 1.0, 0.0).astype(BF16)

    def body(c, carry):
        r0 = pl.multiple_of(c * tc, tc)
        x = x_ref[0, pl.ds(r0, tc), :]
        acc = jnp.zeros((tc, LANES), F32)
        for part in _split3(x):
            acc = acc + jnp.dot(tri, part, preferred_element_type=F32)
        acc = acc + carry
        o_ref[0, pl.ds(r0, tc), :] = acc
        return acc[tc - 1:tc, :]

    lax.fori_loop(0, seq // tc, body, jnp.zeros((1, LANES), F32))


def cumsum_rows(x, name):
    b, seq, n = x.shape
    tc = _pick(seq, 256)
    return pl.pallas_call(
        functools.partial(_cumsum_kernel, seq=seq, tc=tc),
        out_shape=jax.ShapeDtypeStruct((b, seq, n), F32),
        grid=(b,),
        in_specs=[pl.BlockSpec((1, seq, n), lambda i: (i, 0, 0))],
        out_specs=pl.BlockSpec((1, seq, n), lambda i: (i, 0, 0)),
        compiler_params=_cparams(("parallel",)),
        name=name,
    )(x)


def _dot_nt(a, b):
    return lax.dot_general(a, b, (((1,), (1,)), ((), ())), preferred_element_type=F32)


def _softmax_chunk(s, v, carry):
    m, l, acc = carry
    m_new = jnp.maximum(m, jnp.max(s, axis=-1, keepdims=True))
    alpha = jnp.exp(m - m_new)
    p = jnp.exp(s - m_new)
    l = alpha * l + jnp.sum(p, axis=-1, keepdims=True)
    acc = alpha * acc + jnp.dot(p.astype(v.dtype), v, preferred_element_type=F32)
    return m_new, l, acc


def _softmax_init(tq):
    return (jnp.full((tq, 1), NEG_BIG, F32), jnp.zeros((tq, 1), F32),
            jnp.zeros((tq, HEAD_DIM), F32))


def _mla_kernel(qn_ref, qr_ref, kn_ref, kr_ref, v_ref, o_ref, *, t, scale):
    qi = pl.program_id(2)
    q = jnp.concatenate([qn_ref[...], qr_ref[...]], axis=-1)

    def scores(k0):
        k = jnp.concatenate([kn_ref[pl.ds(k0, t), :], kr_ref[pl.ds(k0, t), :]], axis=-1)
        return _dot_nt(q, k) * scale

    def body(j, carry):
        k0 = pl.multiple_of(j * t, t)
        return _softmax_chunk(scores(k0), v_ref[pl.ds(k0, t), :], carry)

    carry = lax.fori_loop(0, qi, body, _softmax_init(t))
    k0 = pl.multiple_of(qi * t, t)
    rows = lax.broadcasted_iota(jnp.int32, (t, t), 0) // CHUNK
    cols = lax.broadcasted_iota(jnp.int32, (t, t), 1) // CHUNK
    s = jnp.where(cols <= rows, scores(k0), NEG_BIG)
    _, l, acc = _softmax_chunk(s, v_ref[pl.ds(k0, t), :], carry)
    o_ref[...] = (acc / l).astype(o_ref.dtype)


def mla_attention(q, kv, kr, batch, seq, heads, name):
    m = q.shape[0]
    t = _pick(seq, 256)
    nq = seq // t
    scale = 1.0 / math.sqrt(HEAD_DIM + ROPE)
    return pl.pallas_call(
        functools.partial(_mla_kernel, t=t, scale=scale),
        out_shape=jax.ShapeDtypeStruct((m, heads * HEAD_DIM), BF16),
        grid=(batch, heads, nq),
        in_specs=[pl.BlockSpec((t, HEAD_DIM), lambda b, h, i: (b * nq + i, h)),
                  pl.BlockSpec((t, HEAD_DIM), lambda b, h, i: (b * nq + i, heads + h)),
                  pl.BlockSpec((seq, HEAD_DIM), lambda b, h, i: (b, h)),
                  pl.BlockSpec((seq, LANES), lambda b, h, i: (b, 0)),
                  pl.BlockSpec((seq, HEAD_DIM), lambda b, h, i: (b, heads + h))],
        out_specs=pl.BlockSpec((t, HEAD_DIM), lambda b, h, i: (b * nq + i, h)),
        compiler_params=_cparams(("parallel", "parallel", "arbitrary")),
        name=name,
    )(q, q, kv, kr, kv)


def _fox_kernel(q_ref, k_ref, v_ref, crow_ref, ccol_ref, o_ref, *, t, scale):
    qi = pl.program_id(2)
    q = q_ref[...]
    cq = ccol_ref[0]

    def scores(j, k0):
        s = _dot_nt(q, k_ref[pl.ds(k0, t), :]) * scale
        return s + cq - crow_ref[0, j]

    def body(j, carry):
        k0 = pl.multiple_of(j * t, t)
        return _softmax_chunk(scores(j, k0), v_ref[pl.ds(k0, t), :], carry)

    carry = lax.fori_loop(0, qi, body, _softmax_init(t))
    k0 = pl.multiple_of(qi * t, t)
    rows = lax.broadcasted_iota(jnp.int32, (t, t), 0)
    cols = lax.broadcasted_iota(jnp.int32, (t, t), 1)
    s = jnp.where(cols <= rows, scores(qi, k0), NEG_BIG)
    _, l, acc = _softmax_chunk(s, v_ref[pl.ds(k0, t), :], carry)
    o_ref[...] = (acc / l).astype(o_ref.dtype)


def fox_attention(qkv, col0, c_row, c_col, batch, seq, heads, name):
    m = qkv.shape[0]
    t = _pick(seq, 256)
    nq = seq // t
    scale = 1.0 / math.sqrt(HEAD_DIM)
    return pl.pallas_call(
        functools.partial(_fox_kernel, t=t, scale=scale),
        out_shape=jax.ShapeDtypeStruct((m, heads * HEAD_DIM), BF16),
        grid=(batch, heads, nq),
        in_specs=[pl.BlockSpec((t, HEAD_DIM), lambda b, h, i: (b * nq + i, col0 + h)),
                  pl.BlockSpec((seq, HEAD_DIM), lambda b, h, i: (b, col0 + heads + h)),
                  pl.BlockSpec((seq, HEAD_DIM), lambda b, h, i: (b, col0 + 2 * heads + h)),
                  pl.BlockSpec((1, nq, 1, t), lambda b, h, i: (b * heads + h, 0, 0, 0)),
                  pl.BlockSpec((1, t, 1), lambda b, h, i: (b * heads + h, i, 0))],
        out_specs=pl.BlockSpec((t, HEAD_DIM), lambda b, h, i: (b * nq + i, h)),
        compiler_params=_cparams(("parallel", "parallel", "arbitrary")),
        name=name,
    )(qkv, qkv, qkv, c_row, c_col)


def _sb_kernel(q_ref, k_ref, v_ref, o_ref, *, t, scale):
    qi = pl.program_id(2)
    q = q_ref[...]
    rows = lax.broadcasted_iota(jnp.int32, (t, t), 0)
    cols = lax.broadcasted_iota(jnp.int32, (t, t), 1)
    suffix = jnp.where(rows >= cols, 1.0, 0.0).astype(BF16)

    def chunk(k0, carry, strict):
        rest, acc = carry
        z = _dot_nt(q, k_ref[pl.ds(k0, t), :]) * scale
        lom = _log_sigmoid(-z)
        if strict is not None:
            lom = jnp.where(strict, lom, 0.0)
        inc = jnp.zeros((t, t), F32)
        for part in _split3(lom):
            inc = inc + jnp.dot(part, suffix, preferred_element_type=F32)
        a = jnp.exp(z + inc + rest)
        if strict is not None:
            a = jnp.where(strict, a, 0.0)
        acc = acc + jnp.dot(a.astype(v_ref.dtype), v_ref[pl.ds(k0, t), :],
                            preferred_element_type=F32)
        return rest + inc[:, 0:1], acc

    k0 = pl.multiple_of(qi * t, t)
    carry = chunk(k0, (jnp.zeros((t, 1), F32), jnp.zeros((t, HEAD_DIM), F32)), cols < rows)

    def body(j, carry):
        kb = pl.multiple_of((qi - 1 - j) * t, t)
        return chunk(kb, carry, None)

    _, acc = lax.fori_loop(0, qi, body, carry)
    o_ref[...] = acc.astype(o_ref.dtype)


def sb_attention(qkv, col0, batch, seq, heads, name):
    m = qkv.shape[0]
    t = _pick(seq, 256)
    nq = seq // t
    scale = 1.0 / math.sqrt(HEAD_DIM)
    return pl.pallas_call(
        functools.partial(_sb_kernel, t=t, scale=scale),
        out_shape=jax.ShapeDtypeStruct((m, heads * HEAD_DIM), BF16),
        grid=(batch, heads, nq),
        in_specs=[pl.BlockSpec((t, HEAD_DIM), lambda b, h, i: (b * nq + i, col0 + h)),
                  pl.BlockSpec((seq, HEAD_DIM), lambda b, h, i: (b, col0 + heads + h)),
                  pl.BlockSpec((seq, HEAD_DIM), lambda b, h, i: (b, col0 + 2 * heads + h))],
        out_specs=pl.BlockSpec((t, HEAD_DIM), lambda b, h, i: (b * nq + i, h)),
        compiler_params=_cparams(("parallel", "parallel", "arbitrary")),
        name=name,
    )(qkv, qkv, qkv)


def _merge_kernel(oa_ref, ob_ref, oc_ref, wa_ref, wb_ref, wc_ref,
                  ga_ref, gb_ref, gc_ref, o_ref):
    acc = ga_ref[...].astype(F32) * jnp.dot(oa_ref[...], wa_ref[...], preferred_element_type=F32)
    acc += gb_ref[...].astype(F32) * jnp.dot(ob_ref[...], wb_ref[...], preferred_element_type=F32)
    acc += gc_ref[...].astype(F32) * jnp.dot(oc_ref[...], wc_ref[...], preferred_element_type=F32)
    o_ref[...] = acc.astype(o_ref.dtype)


def gated_merge(oa, ob, oc, wa, wb, wc, gates, name):
    m = oa.shape[0]
    d = wa.shape[1]
    tm, tn = _pick(m, 1024), _pick(d, 512)
    nj = d // tn
    row = lambda a: pl.BlockSpec((tm, a.shape[1]), lambda i, j: (i, 0))
    col = lambda a: pl.BlockSpec((a.shape[0], tn), lambda i, j: (0, j))
    gate = lambda g: pl.BlockSpec((tm, tn), lambda i, j: (i, g * nj + j))
    return pl.pallas_call(
        _merge_kernel,
        out_shape=jax.ShapeDtypeStruct((m, d), BF16),
        grid=(m // tm, nj),
        in_specs=[row(oa), row(ob), row(oc), col(wa), col(wb), col(wc),
                  gate(0), gate(1), gate(2)],
        out_specs=pl.BlockSpec((tm, tn), lambda i, j: (i, j)),
        compiler_params=_cparams(("parallel", "parallel")),
        name=name,
    )(oa, ob, oc, wa, wb, wc, gates, gates, gates)


def _ffn_up_kernel(h_ref, wg_ref, wu_ref, cw_ref, cb_ref, o_ref, carry_ref, *, tiles_per_seq):
    i = pl.program_id(1)
    g = jnp.dot(h_ref[...], wg_ref[...], preferred_element_type=F32)
    u = jnp.dot(h_ref[...], wu_ref[...], preferred_element_type=F32)
    tm = g.shape[0]
    @pl.when(i % tiles_per_seq == 0)
    def _():
        carry_ref[...] = jnp.zeros_like(carry_ref)

    prev = carry_ref[...]
    carry_ref[...] = g[tm - 8:, :]
    row = lax.broadcasted_iota(jnp.int32, g.shape, 0)
    g1 = jnp.where(row == 0, prev[7:8, :], pltpu.roll(g, 1, 0))
    g2 = jnp.where(row == 0, prev[6:7, :],
                   jnp.where(row == 1, prev[7:8, :], pltpu.roll(g, 2, 0)))
    cw = cw_ref[...]
    conv = cb_ref[...] + cw[0:1, :] * g2 + cw[1:2, :] * g1 + cw[2:3, :] * g
    act = conv * (1.0 / (1.0 + jnp.exp(-conv)))
    o_ref[...] = (act * u).astype(o_ref.dtype)


def ffn_up(h, wg, wu, conv_w, conv_b, seq, name):
    m, d = h.shape
    f = wg.shape[1]
    tm, tn = _pick(seq, 1024), _pick(f, 512)
    return pl.pallas_call(
        functools.partial(_ffn_up_kernel, tiles_per_seq=seq // tm),
        out_shape=jax.ShapeDtypeStruct((m, f), BF16),
        grid=(f // tn, m // tm),
        in_specs=[pl.BlockSpec((tm, d), lambda j, i: (i, 0)),
                  pl.BlockSpec((d, tn), lambda j, i: (0, j)),
                  pl.BlockSpec((d, tn), lambda j, i: (0, j)),
                  pl.BlockSpec((CONV_WIDTH, tn), lambda j, i: (0, j)),
                  pl.BlockSpec((1, tn), lambda j, i: (0, j))],
        out_specs=pl.BlockSpec((tm, tn), lambda j, i: (i, j)),
        scratch_shapes=[pltpu.VMEM((8, tn), F32)],
        compiler_params=_cparams(("arbitrary", "arbitrary")),
        name=name,
    )(h, wg, wu, conv_w, conv_b.reshape(1, f))


def _rope_tables(seq):
    inv = 1.0 / (ROPE_THETA ** (jnp.arange(0, ROPE, 2, dtype=F32) / ROPE))
    ang = jnp.arange(seq, dtype=F32)[:, None] * inv[None, :]
    zeros = jnp.zeros((seq, LANES - ROPE), F32)
    cos_t = jnp.concatenate([jnp.cos(ang), jnp.cos(ang), zeros], axis=-1)
    sin_t = jnp.concatenate([jnp.sin(ang), jnp.sin(ang), zeros], axis=-1)
    return cos_t, sin_t


def _rope_partner(w):
    half = ROPE // 2
    return jnp.concatenate([-w[..., half:], w[..., :half]], axis=-1)


def _layer_weights(d, w_in, w_uq, w_ukv):
    ql, kvl = d // 4, d // 8
    hm, hf = d // (2 * HEAD_DIM), d // (4 * HEAD_DIM)
    fw = hf * HEAD_DIM
    o_kv = ql
    o_kr = o_kv + kvl
    o_f = o_kr + ROPE
    o_fpre = o_f + 3 * fw
    o_s = o_fpre + hf
    o_g = o_s + 3 * fw
    wb = w_in.astype(BF16)
    zpad = jnp.zeros((d, LANES - ROPE), BF16)
    kr = wb[:, o_kr:o_kr + ROPE]
    w_small = jnp.concatenate(
        [wb[:, o_kv:o_kv + kvl], kr, zpad, _rope_partner(kr), zpad,
         wb[:, o_fpre:o_fpre + hf], jnp.zeros((d, LANES - hf), BF16)], axis=-1)
    w_qkv = jnp.concatenate([wb[:, o_f:o_f + 3 * fw], wb[:, o_s:o_s + 3 * fw]], axis=-1)
    uq = w_uq.astype(BF16).reshape(ql, hm, HEAD_DIM + ROPE)
    r = uq[:, :, HEAD_DIM:]
    z = jnp.zeros((ql, hm, LANES - ROPE), BF16)
    w_q = jnp.concatenate(
        [uq[:, :, :HEAD_DIM].reshape(ql, hm * HEAD_DIM),
         jnp.concatenate([r, z], axis=-1).reshape(ql, hm * HEAD_DIM),
         jnp.concatenate([_rope_partner(r), z], axis=-1).reshape(ql, hm * HEAD_DIM)], axis=-1)
    ukv = w_ukv.astype(BF16).reshape(kvl, hm, 2 * HEAD_DIM)
    w_kv = jnp.concatenate([ukv[:, :, :HEAD_DIM].reshape(kvl, hm * HEAD_DIM),
                            ukv[:, :, HEAD_DIM:].reshape(kvl, hm * HEAD_DIM)], axis=-1)
    return dict(q_lat=wb[:, :ql], small=w_small, qkv=w_qkv, gate=wb[:, o_g:], uq=w_q, ukv=w_kv)


def kernel(x, attn_norm, w_in, b_forget, b_gate, q_norm, w_uq, kv_norm, w_ukv, w_br_mla, w_br_fox, w_br_sb, w_o, ffn_norm, w_ffn_gate, conv_w, conv_b, w_ffn_up, w_ffn_down, final_norm):
    batch, seq, d = x.shape
    depth = w_in.shape[0]
    m = batch * seq
    hm, hf = d // (2 * HEAD_DIM), d // (4 * HEAD_DIM)
    hs = hf
    fcols = hf * 3
    t_att = _pick(seq, 256)
    nq = seq // t_att
    cos_t, sin_t = _rope_tables(seq)
    xf = x.reshape(m, d)

    for l in range(depth):
        w = _layer_weights(d, w_in[l], w_uq[l], w_ukv[l])
        h = rmsnorm(xf, attn_norm[l], BF16, f"attn_norm_{l}")

        qn = matmul(h, w["q_lat"], out_dtype=BF16, name=f"inproj_qlat_{l}",
                    epilogue=_ep_rmsnorm, extras=[(q_norm[l].reshape(1, -1), "row")],
                    tn=w["q_lat"].shape[1])
        bf_pad = jnp.zeros((1, LANES), F32).at[0, :hf].set(b_forget[l])
        kvn, kr, logf = inproj_small(h, w["small"], kv_norm[l], bf_pad, cos_t, sin_t, seq,
                                     f"inproj_small_{l}")
        qkv = matmul(h, w["qkv"], out_dtype=BF16, name=f"inproj_qkv_{l}")
        gates = matmul(h, w["gate"], out_dtype=BF16, name=f"inproj_gate_{l}",
                       epilogue=_ep_sigmoid_bias, extras=[(b_gate[l].reshape(1, -1), "row")])

        q = mla_q(qn, w["uq"], cos_t, sin_t, seq, hm, f"mla_q_{l}")
        kv = matmul(kvn, w["ukv"], out_dtype=BF16, name=f"mla_kv_{l}")
        o_mla = mla_attention(q, kv, kr, batch, seq, hm, f"mla_attn_{l}")

        c = cumsum_rows(logf.reshape(batch, seq, LANES), f"fox_cumsum_{l}")
        c_heads = jnp.transpose(c[:, :, :hf], (0, 2, 1)).reshape(batch * hf, seq)
        o_fox = fox_attention(qkv, 0, c_heads.reshape(batch * hf, nq, 1, t_att),
                              c_heads.reshape(batch * hf, seq, 1), batch, seq, hf,
                              f"fox_attn_{l}")

        o_sb = sb_attention(qkv, fcols, batch, seq, hs, f"sb_attn_{l}")

        merged = gated_merge(o_mla, o_fox, o_sb, w_br_mla[l].astype(BF16),
                             w_br_fox[l].astype(BF16), w_br_sb[l].astype(BF16), gates,
                             f"merge_{l}")
        xf = matmul(merged, w_o[l].astype(BF16), out_dtype=F32, name=f"out_proj_{l}",
                    epilogue=_ep_residual, extras=[(xf, "tile")])

        h = rmsnorm(xf, ffn_norm[l], BF16, f"ffn_norm_{l}")
        act = ffn_up(h, w_ffn_gate[l].astype(BF16), w_ffn_up[l].astype(BF16),
                     conv_w[l], conv_b[l], seq, f"ffn_up_{l}")
        xf = matmul(act, w_ffn_down[l].astype(BF16), out_dtype=F32, name=f"ffn_down_{l}",
                    epilogue=_ep_residual, extras=[(xf, "tile")], tk=4096)

    return rmsnorm(xf, final_norm, F32, "final_norm").reshape(batch, seq, d)
```

```python
import functools
import math

import jax
import jax.numpy as jnp
from jax import lax
from jax.experimental import pallas as pl
from jax.experimental.pallas import tpu as pltpu

HEAD_DIM = 128
CHUNK = 64
ROPE = 64
ROPE_THETA = 10000.0
CONV_WIDTH = 3
EPS = 1e-6
LANES = 128
NEG_BIG = -1e30
LOG2E = math.log2(math.e)
VMEM_LIMIT = 56 * 1024 * 1024
ATT_TILE = 256
ATT_HEADS = 4

F32 = jnp.float32
BF16 = jnp.bfloat16


def _cparams(sem):
    return pltpu.CompilerParams(dimension_semantics=sem, vmem_limit_bytes=VMEM_LIMIT)


def _pick(n, pref):
    t = min(n, pref)
    while n % t:
        t //= 2
    return t


def _head_group(heads):
    g = ATT_HEADS
    while heads % g:
        g //= 2
    return g


def _rms(x, g):
    return x * lax.rsqrt(jnp.mean(x * x, axis=-1, keepdims=True) + EPS) * g


def _sigmoid(x):
    return 1.0 / (1.0 + jnp.exp(-x))


def _log_sigmoid(x):
    return jnp.minimum(x, 0.0) - jnp.log(1.0 + jnp.exp(-jnp.abs(x)))


def _rmsnorm_kernel(x_ref, g_ref, o_ref):
    o_ref[...] = _rms(x_ref[...], g_ref[...]).astype(o_ref.dtype)


def rmsnorm(x, g, out_dtype, name):
    m, d = x.shape
    tm = _pick(m, 256)
    return pl.pallas_call(
        _rmsnorm_kernel,
        out_shape=jax.ShapeDtypeStruct((m, d), out_dtype),
        grid=(m // tm,),
        in_specs=[pl.BlockSpec((tm, d), lambda i: (i, 0)),
                  pl.BlockSpec((1, d), lambda i: (0, 0))],
        out_specs=pl.BlockSpec((tm, d), lambda i: (i, 0)),
        compiler_params=_cparams(("parallel",)),
        name=name,
    )(x, g.reshape(1, d))


def _mm_kernel(*refs, nk, n_extra, epilogue):
    x_ref, w_ref = refs[0], refs[1]
    extras = refs[2:2 + n_extra]
    o_ref = refs[2 + n_extra]
    if nk == 1:
        acc = jnp.dot(x_ref[...], w_ref[...], preferred_element_type=F32)
        o_ref[...] = epilogue(acc, *extras).astype(o_ref.dtype)
        return
    acc_ref = refs[3 + n_extra]
    k = pl.program_id(2)

    @pl.when(k == 0)
    def _():
        acc_ref[...] = jnp.dot(x_ref[...], w_ref[...], preferred_element_type=F32)

    @pl.when(k > 0)
    def _():
        acc_ref[...] += jnp.dot(x_ref[...], w_ref[...], preferred_element_type=F32)

    @pl.when(k == nk - 1)
    def _():
        o_ref[...] = epilogue(acc_ref[...], *extras).astype(o_ref.dtype)


def matmul(x, w, *, out_dtype, name, epilogue=None, extras=(), tm=1024, tn=512, tk=None):
    m, kd = x.shape
    _, n = w.shape
    tm, tn = _pick(m, tm), _pick(n, tn)
    tk = kd if tk is None else _pick(kd, tk)
    nk = kd // tk
    if epilogue is None:
        epilogue = lambda acc: acc
    in_specs = [pl.BlockSpec((tm, tk), lambda i, j, k: (i, k)),
                pl.BlockSpec((tk, tn), lambda i, j, k: (k, j))]
    args = [x, w]
    for arr, kind in extras:
        if kind == "row":
            in_specs.append(pl.BlockSpec((1, tn), lambda i, j, k: (0, j)))
        else:
            in_specs.append(pl.BlockSpec((tm, tn), lambda i, j, k: (i, j)))
        args.append(arr)
    scratch = [pltpu.VMEM((tm, tn), F32)] if nk > 1 else []
    return pl.pallas_call(
        functools.partial(_mm_kernel, nk=nk, n_extra=len(extras), epilogue=epilogue),
        out_shape=jax.ShapeDtypeStruct((m, n), out_dtype),
        grid=(m // tm, n // tn, nk),
        in_specs=in_specs,
        out_specs=pl.BlockSpec((tm, tn), lambda i, j, k: (i, j)),
        scratch_shapes=scratch,
        compiler_params=_cparams(("parallel", "parallel", "arbitrary")),
        name=name,
    )(*args)


def _ep_rmsnorm(acc, g_ref):
    return _rms(acc, g_ref[...])


def _ep_sigmoid_bias(acc, b_ref):
    return _sigmoid(acc + b_ref[...])


def _ep_residual(acc, r_ref):
    return r_ref[...] + acc


def _ep_colscale(acc, s_ref):
    return acc * s_ref[...]


def _dot_nt(a, b):
    return lax.dot_general(a, b, (((1,), (1,)), ((), ())), preferred_element_type=F32)


def _mm_t_kernel(x_ref, wt_ref, o_ref, *, t):
    res = _dot_nt(wt_ref[...], x_ref[...])
    for c in range(o_ref.shape[0]):
        o_ref[c] = res[:, c * t:(c + 1) * t].astype(o_ref.dtype)


def matmul_t(x, wt, t, name):
    m, kd = x.shape
    n = wt.shape[0]
    tm, tn = _pick(m, 1024), _pick(n, 512)
    return pl.pallas_call(
        functools.partial(_mm_t_kernel, t=t),
        out_shape=jax.ShapeDtypeStruct((m // t, n, t), BF16),
        grid=(n // tn, m // tm),
        in_specs=[pl.BlockSpec((tm, kd), lambda j, i: (i, 0)),
                  pl.BlockSpec((tn, kd), lambda j, i: (j, 0))],
        out_specs=pl.BlockSpec((tm // t, tn, t), lambda j, i: (i, j, 0)),
        compiler_params=_cparams(("parallel", "parallel")),
        name=name,
    )(x, wt)


def _small_kernel(h_ref, w_ref, g_ref, bf_ref, cos_ref, sin_ref,
                  kvn_ref, kr_ref, lf_ref, *, kvl):
    acc = jnp.dot(h_ref[...], w_ref[...], preferred_element_type=F32)
    kvn_ref[...] = _rms(acc[:, :kvl], g_ref[...]).astype(kvn_ref.dtype)
    r = acc[:, kvl:kvl + LANES]
    p = acc[:, kvl + LANES:kvl + 2 * LANES]
    kr_ref[...] = (r * cos_ref[...] + p * sin_ref[...]).astype(kr_ref.dtype)
    lf_ref[...] = _log_sigmoid(acc[:, kvl + 2 * LANES:kvl + 3 * LANES] + bf_ref[...])


def inproj_small(h, w, kv_norm, b_forget_pad, cos_t, sin_t, seq, name):
    m, d = h.shape
    kvl = kv_norm.shape[0]
    n = w.shape[1]
    tm = _pick(seq, 512)
    nseq = seq // tm
    return pl.pallas_call(
        functools.partial(_small_kernel, kvl=kvl),
        out_shape=(jax.ShapeDtypeStruct((m, kvl), BF16),
                   jax.ShapeDtypeStruct((m, LANES), BF16),
                   jax.ShapeDtypeStruct((m, LANES), F32)),
        grid=(m // tm,),
        in_specs=[pl.BlockSpec((tm, d), lambda i: (i, 0)),
                  pl.BlockSpec((d, n), lambda i: (0, 0)),
                  pl.BlockSpec((1, kvl), lambda i: (0, 0)),
                  pl.BlockSpec((1, LANES), lambda i: (0, 0)),
                  pl.BlockSpec((tm, LANES), lambda i: (i % nseq, 0)),
                  pl.BlockSpec((tm, LANES), lambda i: (i % nseq, 0))],
        out_specs=(pl.BlockSpec((tm, kvl), lambda i: (i, 0)),
                   pl.BlockSpec((tm, LANES), lambda i: (i, 0)),
                   pl.BlockSpec((tm, LANES), lambda i: (i, 0))),
        compiler_params=_cparams(("parallel",)),
        name=name,
    )(h, w, kv_norm.reshape(1, kvl), b_forget_pad, cos_t, sin_t)


def _mlaq_kernel(x_ref, w_ref, cos_ref, sin_ref, o_ref, *, heads, qscale):
    hw = heads * HEAD_DIM
    x = x_ref[...]
    nope = jnp.dot(x, w_ref[:, :hw], preferred_element_type=F32)
    o_ref[:, :hw] = (nope * qscale).astype(o_ref.dtype)
    r = jnp.dot(x, w_ref[:, hw:2 * hw], preferred_element_type=F32)
    p = jnp.dot(x, w_ref[:, 2 * hw:], preferred_element_type=F32)
    cos_t, sin_t = cos_ref[...] * qscale, sin_ref[...] * qscale
    for h in range(heads):
        sl = slice(h * HEAD_DIM, (h + 1) * HEAD_DIM)
        o_ref[:, hw + h * HEAD_DIM:hw + (h + 1) * HEAD_DIM] = (
            r[:, sl] * cos_t + p[:, sl] * sin_t).astype(o_ref.dtype)


def mla_q(qn, w, cos_t, sin_t, seq, heads, qscale, name):
    m, ql = qn.shape
    n = w.shape[1]
    tm = _pick(seq, 256)
    nseq = seq // tm
    return pl.pallas_call(
        functools.partial(_mlaq_kernel, heads=heads, qscale=qscale),
        out_shape=jax.ShapeDtypeStruct((m, 2 * heads * HEAD_DIM), BF16),
        grid=(m // tm,),
        in_specs=[pl.BlockSpec((tm, ql), lambda i: (i, 0)),
                  pl.BlockSpec((ql, n), lambda i: (0, 0)),
                  pl.BlockSpec((tm, LANES), lambda i: (i % nseq, 0)),
                  pl.BlockSpec((tm, LANES), lambda i: (i % nseq, 0))],
        out_specs=pl.BlockSpec((tm, 2 * heads * HEAD_DIM), lambda i: (i, 0)),
        compiler_params=_cparams(("parallel",)),
        name=name,
    )(qn, w, cos_t, sin_t)


def _split(x, parts):
    out = []
    for _ in range(parts - 1):
        hi = x.astype(BF16)
        out.append(hi)
        x = x - hi.astype(F32)
    out.append(x.astype(BF16))
    return out


def _cumsum_kernel(x_ref, o_ref, *, seq, tc):
    rows = lax.broadcasted_iota(jnp.int32, (tc, tc), 0)
    cols = lax.broadcasted_iota(jnp.int32, (tc, tc), 1)
    tri = jnp.where(cols <= rows, 1.0, 0.0).astype(BF16)

    def body(c, carry):
        r0 = pl.multiple_of(c * tc, tc)
        x = x_ref[0, pl.ds(r0, tc), :]
        acc = jnp.zeros((tc, LANES), F32)
        for part in _split(x, 3):
            acc = acc + jnp.dot(tri, part, preferred_element_type=F32)
        acc = acc + carry
        o_ref[0, pl.ds(r0, tc), :] = acc * LOG2E
        return acc[tc - 1:tc, :]

    lax.fori_loop(0, seq // tc, body, jnp.zeros((1, LANES), F32))


def cumsum_rows(x, name):
    b, seq, n = x.shape
    tc = _pick(seq, 256)
    return pl.pallas_call(
        functools.partial(_cumsum_kernel, seq=seq, tc=tc),
        out_shape=jax.ShapeDtypeStruct((b, seq, n), F32),
        grid=(b,),
        in_specs=[pl.BlockSpec((1, seq, n), lambda i: (i, 0, 0))],
        out_specs=pl.BlockSpec((1, seq, n), lambda i: (i, 0, 0)),
        compiler_params=_cparams(("parallel",)),
        name=name,
    )(x)


def _softmax_update(ss, vts, carry):
    ps, stats = [], []
    for s, (m, l, _) in zip(ss, carry):
        m_new = jnp.maximum(m, jnp.max(s, axis=0, keepdims=True))
        alpha = jnp.exp2(m - m_new)
        p = jnp.exp2(s - m_new)
        stats.append((m_new, alpha * l + jnp.sum(p, axis=0, keepdims=True), alpha))
        ps.append(p.astype(BF16))
    pvs = [jnp.dot(vt, p, preferred_element_type=F32) for vt, p in zip(vts, ps)]
    return tuple((m, l, alpha * c[2] + pv) for (m, l, alpha), c, pv in zip(stats, carry, pvs))


def _softmax_init(t, g):
    return tuple((jnp.full((1, t), NEG_BIG, F32), jnp.zeros((1, t), F32),
                  jnp.zeros((HEAD_DIM, t), F32)) for _ in range(g))


def _softmax_store(o_ref, carry):
    for i, (_, l, acc) in enumerate(carry):
        o_ref[:, i * HEAD_DIM:(i + 1) * HEAD_DIM] = (acc * (1.0 / l)).T.astype(o_ref.dtype)


def _head(ref, i, rows=slice(None)):
    return ref[rows, i * HEAD_DIM:(i + 1) * HEAD_DIM]


def _mla_kernel(qn_ref, qr_ref, kn_ref, kr_ref, vt_ref, o_ref, *, t, g):
    qi = pl.program_id(2)
    qs = [jnp.concatenate([_head(qn_ref, i), _head(qr_ref, i)], axis=-1) for i in range(g)]

    def step(j, carry, mask):
        rows = pl.ds(pl.multiple_of(j * t, t), t)
        kr = kr_ref[rows, :]
        ss = [_dot_nt(jnp.concatenate([_head(kn_ref, i, rows), kr], axis=-1), qs[i])
              for i in range(g)]
        if mask is not None:
            ss = [jnp.where(mask, s, NEG_BIG) for s in ss]
        vts = [vt_ref[j, i * HEAD_DIM:(i + 1) * HEAD_DIM, :] for i in range(g)]
        return _softmax_update(ss, vts, carry)

    carry = lax.fori_loop(0, qi, lambda j, c: step(j, c, None), _softmax_init(t, g))
    keys = lax.broadcasted_iota(jnp.int32, (t, t), 0) // CHUNK
    queries = lax.broadcasted_iota(jnp.int32, (t, t), 1) // CHUNK
    _softmax_store(o_ref, step(qi, carry, keys <= queries))


def mla_attention(q, kn, kr, vt, batch, seq, heads, name):
    m = q.shape[0]
    t = vt.shape[2]
    nq = seq // t
    g = _head_group(heads)
    gw, hb = g * HEAD_DIM, heads // g
    return pl.pallas_call(
        functools.partial(_mla_kernel, t=t, g=g),
        out_shape=jax.ShapeDtypeStruct((m, heads * HEAD_DIM), BF16),
        grid=(batch, hb, nq),
        in_specs=[pl.BlockSpec((t, gw), lambda b, h, i: (b * nq + i, h)),
                  pl.BlockSpec((t, gw), lambda b, h, i: (b * nq + i, hb + h)),
                  pl.BlockSpec((seq, gw), lambda b, h, i: (b, h)),
                  pl.BlockSpec((seq, LANES), lambda b, h, i: (b, 0)),
                  pl.BlockSpec((nq, gw, t), lambda b, h, i: (b, h, 0))],
        out_specs=pl.BlockSpec((t, gw), lambda b, h, i: (b * nq + i, h)),
        compiler_params=_cparams(("parallel", "parallel", "arbitrary")),
        name=name,
    )(q, q, kn, kr, vt)


def _fox_kernel(q_ref, k_ref, vt_ref, cq_ref, ck_ref, o_ref, *, t, g):
    qi = pl.program_id(2)
    qs = [_head(q_ref, i) for i in range(g)]
    cqs = [cq_ref[i] for i in range(g)]

    def step(j, carry, mask):
        rows = pl.ds(pl.multiple_of(j * t, t), t)
        ss = []
        for i in range(g):
            ck = ck_ref[i, rows, :]
            ck = jnp.concatenate([ck] * (t // LANES), axis=1)
            s = _dot_nt(_head(k_ref, i, rows), qs[i]) + cqs[i] - ck
            ss.append(s if mask is None else jnp.where(mask, s, NEG_BIG))
        vts = [vt_ref[j, i * HEAD_DIM:(i + 1) * HEAD_DIM, :] for i in range(g)]
        return _softmax_update(ss, vts, carry)

    carry = lax.fori_loop(0, qi, lambda j, c: step(j, c, None), _softmax_init(t, g))
    keys = lax.broadcasted_iota(jnp.int32, (t, t), 0)
    queries = lax.broadcasted_iota(jnp.int32, (t, t), 1)
    _softmax_store(o_ref, step(qi, carry, keys <= queries))


def fox_attention(qk, qcol, kcol, vt, vrow, c_row, c_rep, batch, seq, heads, name):
    m = qk.shape[0]
    t = vt.shape[2]
    nq = seq // t
    g = _head_group(heads)
    gw, hb = g * HEAD_DIM, heads // g
    return pl.pallas_call(
        functools.partial(_fox_kernel, t=t, g=g),
        out_shape=jax.ShapeDtypeStruct((m, heads * HEAD_DIM), BF16),
        grid=(batch, hb, nq),
        in_specs=[pl.BlockSpec((t, gw), lambda b, h, i: (b * nq + i, qcol * hb + h)),
                  pl.BlockSpec((seq, gw), lambda b, h, i: (b, kcol * hb + h)),
                  pl.BlockSpec((nq, gw, t), lambda b, h, i: (b, vrow * hb + h, 0)),
                  pl.BlockSpec((g, 1, t), lambda b, h, i: (b * hb + h, 0, i)),
                  pl.BlockSpec((g, seq, LANES), lambda b, h, i: (b * hb + h, 0, 0))],
        out_specs=pl.BlockSpec((t, gw), lambda b, h, i: (b * nq + i, h)),
        compiler_params=_cparams(("parallel", "parallel", "arbitrary")),
        name=name,
    )(qk, qk, vt, c_row, c_rep)


def _sb_kernel(q_ref, k_ref, vt_ref, o_ref, *, t, g):
    qi = pl.program_id(2)
    qs = [_head(q_ref, i) for i in range(g)]
    keys = lax.broadcasted_iota(jnp.int32, (t, t), 0)
    queries = lax.broadcasted_iota(jnp.int32, (t, t), 1)
    upper = jnp.where(queries >= keys, 1.0, 0.0).astype(BF16)

    def step(j, carry, strict):
        rows = pl.ds(pl.multiple_of(j * t, t), t)
        zs = [_dot_nt(_head(k_ref, i, rows), qs[i]) for i in range(g)]
        loms = []
        for z in zs:
            lom = jnp.minimum(-z, 0.0) - jnp.log(1.0 + jnp.exp2(-jnp.abs(z))) * LOG2E
            loms.append(lom if strict is None else jnp.where(strict, lom, 0.0))
        incs = [sum(jnp.dot(upper, part, preferred_element_type=F32) for part in _split(lom, 2))
                for lom in loms]
        avs = []
        for z, inc, (rest, _) in zip(zs, incs, carry):
            a = jnp.exp2(z + inc + rest)
            avs.append((a if strict is None else jnp.where(strict, a, 0.0)).astype(BF16))
        pvs = [jnp.dot(vt_ref[j, i * HEAD_DIM:(i + 1) * HEAD_DIM, :], avs[i],
                       preferred_element_type=F32) for i in range(g)]
        return tuple((rest + inc[0:1, :], acc + pv)
                     for (rest, acc), inc, pv in zip(carry, incs, pvs))

    init = tuple((jnp.zeros((1, t), F32), jnp.zeros((HEAD_DIM, t), F32)) for _ in range(g))
    carry = step(qi, init, keys < queries)
    carry = lax.fori_loop(0, qi, lambda j, c: step(qi - 1 - j, c, None), carry)
    for i, (_, acc) in enumerate(carry):
        o_ref[:, i * HEAD_DIM:(i + 1) * HEAD_DIM] = acc.T.astype(o_ref.dtype)


def sb_attention(qk, qcol, kcol, vt, vrow, batch, seq, heads, name):
    m = qk.shape[0]
    t = vt.shape[2]
    nq = seq // t
    g = _head_group(heads)
    gw, hb = g * HEAD_DIM, heads // g
    return pl.pallas_call(
        functools.partial(_sb_kernel, t=t, g=g),
        out_shape=jax.ShapeDtypeStruct((m, heads * HEAD_DIM), BF16),
        grid=(batch, hb, nq),
        in_specs=[pl.BlockSpec((t, gw), lambda b, h, i: (b * nq + i, qcol * hb + h)),
                  pl.BlockSpec((seq, gw), lambda b, h, i: (b, kcol * hb + h)),
                  pl.BlockSpec((nq, gw, t), lambda b, h, i: (b, vrow * hb + h, 0))],
        out_specs=pl.BlockSpec((t, gw), lambda b, h, i: (b * nq + i, h)),
        compiler_params=_cparams(("parallel", "parallel", "arbitrary")),
        name=name,
    )(qk, qk, vt)


def _merge_kernel(oa_ref, ob_ref, oc_ref, wa_ref, wb_ref, wc_ref,
                  ga_ref, gb_ref, gc_ref, o_ref):
    acc = ga_ref[...].astype(F32) * jnp.dot(oa_ref[...], wa_ref[...], preferred_element_type=F32)
    acc += gb_ref[...].astype(F32) * jnp.dot(ob_ref[...], wb_ref[...], preferred_element_type=F32)
    acc += gc_ref[...].astype(F32) * jnp.dot(oc_ref[...], wc_ref[...], preferred_element_type=F32)
    o_ref[...] = acc.astype(o_ref.dtype)


def gated_merge(oa, ob, oc, wa, wb, wc, gates, name):
    m = oa.shape[0]
    d = wa.shape[1]
    tm, tn = _pick(m, 1024), _pick(d, 512)
    nj = d // tn
    row = lambda a: pl.BlockSpec((tm, a.shape[1]), lambda i, j: (i, 0))
    col = lambda a: pl.BlockSpec((a.shape[0], tn), lambda i, j: (0, j))
    gate = lambda g: pl.BlockSpec((tm, tn), lambda i, j: (i, g * nj + j))
    return pl.pallas_call(
        _merge_kernel,
        out_shape=jax.ShapeDtypeStruct((m, d), BF16),
        grid=(m // tm, nj),
        in_specs=[row(oa), row(ob), row(oc), col(wa), col(wb), col(wc),
                  gate(0), gate(1), gate(2)],
        out_specs=pl.BlockSpec((tm, tn), lambda i, j: (i, j)),
        compiler_params=_cparams(("parallel", "parallel")),
        name=name,
    )(oa, ob, oc, wa, wb, wc, gates, gates, gates)


def _ffn_up_kernel(h_ref, wg_ref, wu_ref, cw_ref, cb_ref, o_ref, carry_ref, *, tiles_per_seq):
    i = pl.program_id(1)
    g = jnp.dot(h_ref[...], wg_ref[...], preferred_element_type=F32)
    u = jnp.dot(h_ref[...], wu_ref[...], preferred_element_type=F32)
    tm = g.shape[0]

    @pl.when(i % tiles_per_seq == 0)
    def _():
        carry_ref[...] = jnp.zeros_like(carry_ref)

    prev = carry_ref[...]
    carry_ref[...] = g[tm - 8:, :]
    row = lax.broadcasted_iota(jnp.int32, g.shape, 0)
    g1 = jnp.where(row == 0, prev[7:8, :], pltpu.roll(g, 1, 0))
    g2 = jnp.where(row == 0, prev[6:7, :],
                   jnp.where(row == 1, prev[7:8, :], pltpu.roll(g, 2, 0)))
    cw = cw_ref[...]
    conv = cb_ref[...] + cw[0:1, :] * g2 + cw[1:2, :] * g1 + cw[2:3, :] * g
    act = conv * (1.0 / (1.0 + jnp.exp(-conv)))
    o_ref[...] = (act * u).astype(o_ref.dtype)


def ffn_up(h, wg, wu, conv_w, conv_b, seq, name):
    m, d = h.shape
    f = wg.shape[1]
    tm, tn = _pick(seq, 1024), _pick(f, 512)
    return pl.pallas_call(
        functools.partial(_ffn_up_kernel, tiles_per_seq=seq // tm),
        out_shape=jax.ShapeDtypeStruct((m, f), BF16),
        grid=(f // tn, m // tm),
        in_specs=[pl.BlockSpec((tm, d), lambda j, i: (i, 0)),
                  pl.BlockSpec((d, tn), lambda j, i: (0, j)),
                  pl.BlockSpec((d, tn), lambda j, i: (0, j)),
                  pl.BlockSpec((CONV_WIDTH, tn), lambda j, i: (0, j)),
                  pl.BlockSpec((1, tn), lambda j, i: (0, j))],
        out_specs=pl.BlockSpec((tm, tn), lambda j, i: (i, j)),
        scratch_shapes=[pltpu.VMEM((8, tn), F32)],
        compiler_params=_cparams(("arbitrary", "arbitrary")),
        name=name,
    )(h, wg, wu, conv_w, conv_b.reshape(1, f))


def _rope_tables(seq):
    inv = 1.0 / (ROPE_THETA ** (jnp.arange(0, ROPE, 2, dtype=F32) / ROPE))
    ang = jnp.arange(seq, dtype=F32)[:, None] * inv[None, :]
    zeros = jnp.zeros((seq, LANES - ROPE), F32)
    cos_t = jnp.concatenate([jnp.cos(ang), jnp.cos(ang), zeros], axis=-1)
    sin_t = jnp.concatenate([jnp.sin(ang), jnp.sin(ang), zeros], axis=-1)
    return cos_t, sin_t


def _rope_partner(w):
    half = ROPE // 2
    return jnp.concatenate([-w[..., half:], w[..., :half]], axis=-1)


def _layer_weights(d, w_in, w_uq, w_ukv):
    ql, kvl = d // 4, d // 8
    hm, hf = d // (2 * HEAD_DIM), d // (4 * HEAD_DIM)
    fw = hf * HEAD_DIM
    o_kv = ql
    o_kr = o_kv + kvl
    o_f = o_kr + ROPE
    o_fpre = o_f + 3 * fw
    o_s = o_fpre + hf
    o_g = o_s + 3 * fw
    zpad = jnp.zeros((d, LANES - ROPE), F32)
    kr = w_in[:, o_kr:o_kr + ROPE]
    w_small = jnp.concatenate(
        [w_in[:, o_kv:o_kv + kvl], kr, zpad, _rope_partner(kr), zpad,
         w_in[:, o_fpre:o_fpre + hf], jnp.zeros((d, LANES - hf), F32)], axis=-1).astype(BF16)
    w_qk = jnp.concatenate([w_in[:, o_f:o_f + 2 * fw], w_in[:, o_s:o_s + 2 * fw]],
                           axis=-1).astype(BF16)
    w_vt = jnp.concatenate([w_in[:, o_f + 2 * fw:o_f + 3 * fw],
                            w_in[:, o_s + 2 * fw:o_s + 3 * fw]], axis=-1).T.astype(BF16)
    uq = w_uq.reshape(ql, hm, HEAD_DIM + ROPE)
    r = uq[:, :, HEAD_DIM:]
    z = jnp.zeros((ql, hm, LANES - ROPE), F32)
    w_q = jnp.concatenate(
        [uq[:, :, :HEAD_DIM].reshape(ql, hm * HEAD_DIM),
         jnp.concatenate([r, z], axis=-1).reshape(ql, hm * HEAD_DIM),
         jnp.concatenate([_rope_partner(r), z], axis=-1).reshape(ql, hm * HEAD_DIM)],
        axis=-1).astype(BF16)
    ukv = w_ukv.reshape(kvl, hm, 2 * HEAD_DIM)
    w_k = ukv[:, :, :HEAD_DIM].reshape(kvl, hm * HEAD_DIM).astype(BF16)
    w_vt_mla = ukv[:, :, HEAD_DIM:].reshape(kvl, hm * HEAD_DIM).T.astype(BF16)
    return dict(q_lat=w_in[:, :ql].astype(BF16), small=w_small, qk=w_qk, vt=w_vt,
                gate=w_in[:, o_g:].astype(BF16), uq=w_q, uk=w_k, uvt=w_vt_mla)


def kernel(x, attn_norm, w_in, b_forget, b_gate, q_norm, w_uq, kv_norm, w_ukv, w_br_mla, w_br_fox, w_br_sb, w_o, ffn_norm, w_ffn_gate, conv_w, conv_b, w_ffn_up, w_ffn_down, final_norm):
    batch, seq, d = x.shape
    depth = w_in.shape[0]
    m = batch * seq
    hm, hf = d // (2 * HEAD_DIM), d // (4 * HEAD_DIM)
    hs = hf
    fw = hf * HEAD_DIM
    t_att = _pick(seq, ATT_TILE)
    cos_t, sin_t = _rope_tables(seq)
    xf = x.reshape(m, d)
    q_scale = LOG2E / math.sqrt(HEAD_DIM)
    mla_scale = LOG2E / math.sqrt(HEAD_DIM + ROPE)
    qk_scale = jnp.concatenate([jnp.full((fw,), q_scale, F32), jnp.ones((fw,), F32)] * 2).reshape(1, -1)

    for l in range(depth):
        w = _layer_weights(d, w_in[l], w_uq[l], w_ukv[l])
        h = rmsnorm(xf, attn_norm[l], BF16, f"attn_norm_{l}")

        qn = matmul(h, w["q_lat"], out_dtype=BF16, name=f"inproj_qlat_{l}",
                    epilogue=_ep_rmsnorm, extras=[(q_norm[l].reshape(1, -1), "row")],
                    tn=w["q_lat"].shape[1])
        bf_pad = jnp.zeros((1, LANES), F32).at[0, :hf].set(b_forget[l])
        kvn, kr, logf = inproj_small(h, w["small"], kv_norm[l], bf_pad, cos_t, sin_t, seq,
                                     f"inproj_small_{l}")
        qk = matmul(h, w["qk"], out_dtype=BF16, name=f"inproj_qk_{l}",
                    epilogue=_ep_colscale, extras=[(qk_scale, "row")], tn=1024)
        vt = matmul_t(h, w["vt"], t_att, f"inproj_vt_{l}")
        gates = matmul(h, w["gate"], out_dtype=BF16, name=f"inproj_gate_{l}",
                       epilogue=_ep_sigmoid_bias, extras=[(b_gate[l].reshape(1, -1), "row")],
                       tn=1024)

        q = mla_q(qn, w["uq"], cos_t, sin_t, seq, hm, mla_scale, f"mla_q_{l}")
        kn = matmul(kvn, w["uk"], out_dtype=BF16, name=f"mla_k_{l}")
        vt_mla = matmul_t(kvn, w["uvt"], t_att, f"mla_vt_{l}")
        o_mla = mla_attention(q, kn, kr, vt_mla, batch, seq, hm, f"mla_attn_{l}")

        c = cumsum_rows(logf.reshape(batch, seq, LANES), f"fox_cumsum_{l}")
        c_heads = jnp.transpose(c[:, :, :hf], (0, 2, 1)).reshape(batch * hf, seq)
        c_rep = jnp.broadcast_to(c_heads[:, :, None], (batch * hf, seq, LANES))
        o_fox = fox_attention(qk, 0, 1, vt, 0, c_heads.reshape(batch * hf, 1, seq), c_rep,
                              batch, seq, hf, f"fox_attn_{l}")

        o_sb = sb_attention(qk, 2, 3, vt, 1, batch, seq, hs, f"sb_attn_{l}")

        merged = gated_merge(o_mla, o_fox, o_sb, w_br_mla[l].astype(BF16),
                             w_br_fox[l].astype(BF16), w_br_sb[l].astype(BF16), gates,
                             f"merge_{l}")
        xf = matmul(merged, w_o[l].astype(BF16), out_dtype=F32, name=f"out_proj_{l}",
                    epilogue=_ep_residual, extras=[(xf, "tile")], tn=1024)

        h = rmsnorm(xf, ffn_norm[l], BF16, f"ffn_norm_{l}")
        act = ffn_up(h, w_ffn_gate[l].astype(BF16), w_ffn_up[l].astype(BF16),
                     conv_w[l], conv_b[l], seq, f"ffn_up_{l}")
        xf = matmul(act, w_ffn_down[l].astype(BF16), out_dtype=F32, name=f"ffn_down_{l}",
                    epilogue=_ep_residual, extras=[(xf, "tile")], tk=4096)

    return rmsnorm(xf, final_norm, F32, "final_norm").reshape(batch, seq, d)
```

```python
import functools
import math

import jax
import jax.numpy as jnp
from jax import lax
from jax.experimental import pallas as pl
from jax.experimental.pallas import tpu as pltpu

HEAD_DIM = 128
CHUNK = 64
ROPE = 64
ROPE_THETA = 10000.0
CONV_WIDTH = 3
EPS = 1e-6
LANES = 128
NEG_BIG = -1e30
LOG2E = math.log2(math.e)
VMEM_LIMIT = 56 * 1024 * 1024
ATT_TILE = 256
ATT_HEADS = 4

F32 = jnp.float32
BF16 = jnp.bfloat16


def _cparams(sem):
    return pltpu.CompilerParams(dimension_semantics=sem, vmem_limit_bytes=VMEM_LIMIT)


def _pick(n, pref):
    t = min(n, pref)
    while n % t:
        t //= 2
    return t


def _head_group(heads):
    g = ATT_HEADS
    while heads % g:
        g //= 2
    return g


def _rms(x, g):
    return x * lax.rsqrt(jnp.mean(x * x, axis=-1, keepdims=True) + EPS) * g


def _sigmoid(x):
    return 1.0 / (1.0 + jnp.exp(-x))


def _log_sigmoid(x):
    return jnp.minimum(x, 0.0) - jnp.log(1.0 + jnp.exp(-jnp.abs(x)))


def _rmsnorm_kernel(x_ref, g_ref, o_ref):
    o_ref[...] = _rms(x_ref[...], g_ref[...]).astype(o_ref.dtype)


def rmsnorm(x, g, out_dtype, name):
    m, d = x.shape
    tm = _pick(m, 256)
    return pl.pallas_call(
        _rmsnorm_kernel,
        out_shape=jax.ShapeDtypeStruct((m, d), out_dtype),
        grid=(m // tm,),
        in_specs=[pl.BlockSpec((tm, d), lambda i: (i, 0)),
                  pl.BlockSpec((1, d), lambda i: (0, 0))],
        out_specs=pl.BlockSpec((tm, d), lambda i: (i, 0)),
        compiler_params=_cparams(("parallel",)),
        name=name,
    )(x, g.reshape(1, d))


def _dot_nt(a, b):
    return lax.dot_general(a, b, (((1,), (1,)), ((), ())), preferred_element_type=F32)


def _mm_kernel(*refs, nk, n_extra, epilogue, wt):
    x_ref, w_ref = refs[0], refs[1]
    extras = refs[2:2 + n_extra]
    o_ref = refs[2 + n_extra]
    if nk == 1:
        if wt:
            acc = _dot_nt(x_ref[...], w_ref[...])
        else:
            acc = jnp.dot(x_ref[...], w_ref[...], preferred_element_type=F32)
        o_ref[...] = epilogue(acc, *extras).astype(o_ref.dtype)
        return
    acc_ref = refs[3 + n_extra]
    k = pl.program_id(2)

    @pl.when(k == 0)
    def _():
        acc_ref[...] = jnp.dot(x_ref[...], w_ref[...], preferred_element_type=F32)

    @pl.when(k > 0)
    def _():
        acc_ref[...] += jnp.dot(x_ref[...], w_ref[...], preferred_element_type=F32)

    @pl.when(k == nk - 1)
    def _():
        o_ref[...] = epilogue(acc_ref[...], *extras).astype(o_ref.dtype)


def matmul(x, w, *, out_dtype, name, epilogue=None, extras=(), tm=1024, tn=512, tk=None,
           col0=0, n_out=None, wt=False):
    m, kd = x.shape
    n = w.shape[0 if wt else 1] - col0 if n_out is None else n_out
    assert not (wt and tk is not None)
    tm, tn = _pick(m, tm), _pick(math.gcd(n, col0), tn)
    cb = col0 // tn
    tk = kd if tk is None else _pick(kd, tk)
    nk = kd // tk
    if epilogue is None:
        epilogue = lambda acc: acc
    in_specs = [pl.BlockSpec((tm, tk), lambda i, j, k: (i, k)),
                pl.BlockSpec((tn, tk), lambda i, j, k: (cb + j, k)) if wt else
                pl.BlockSpec((tk, tn), lambda i, j, k: (k, cb + j))]
    args = [x, w]
    for arr, kind in extras:
        if kind == "row":
            in_specs.append(pl.BlockSpec((1, tn), lambda i, j, k: (0, j)))
        else:
            in_specs.append(pl.BlockSpec((tm, tn), lambda i, j, k: (i, j)))
        args.append(arr)
    scratch = [pltpu.VMEM((tm, tn), F32)] if nk > 1 else []
    return pl.pallas_call(
        functools.partial(_mm_kernel, nk=nk, n_extra=len(extras), epilogue=epilogue, wt=wt),
        out_shape=jax.ShapeDtypeStruct((m, n), out_dtype),
        grid=(m // tm, n // tn, nk),
        in_specs=in_specs,
        out_specs=pl.BlockSpec((tm, tn), lambda i, j, k: (i, j)),
        scratch_shapes=scratch,
        compiler_params=_cparams(("parallel", "parallel", "arbitrary")),
        name=name,
    )(*args)


def _mm_ws_kernel(*refs, n_extra, epilogue):
    x_ref, w_ref = refs[0], refs[1]
    extras = refs[2:2 + n_extra]
    o_ref, w_s = refs[2 + n_extra], refs[3 + n_extra]

    @pl.when(pl.program_id(1) == 0)
    def _():
        w_s[...] = w_ref[...].astype(BF16)

    acc = jnp.dot(x_ref[...], w_s[...], preferred_element_type=F32)
    o_ref[...] = epilogue(acc, *extras).astype(o_ref.dtype)


def matmul_ws(x, w, *, out_dtype, name, epilogue=None, extras=(), tm=1024, tn=512):
    m, kd = x.shape
    _, n = w.shape
    tm, tn = _pick(m, tm), _pick(n, tn)
    if epilogue is None:
        epilogue = lambda acc: acc
    in_specs = [pl.BlockSpec((tm, kd), lambda j, i: (i, 0)),
                pl.BlockSpec((kd, tn), lambda j, i: (0, j))]
    args = [x, w]
    for arr, kind in extras:
        if kind == "row":
            in_specs.append(pl.BlockSpec((1, tn), lambda j, i: (0, j)))
        else:
            in_specs.append(pl.BlockSpec((tm, tn), lambda j, i: (i, j)))
        args.append(arr)
    return pl.pallas_call(
        functools.partial(_mm_ws_kernel, n_extra=len(extras), epilogue=epilogue),
        out_shape=jax.ShapeDtypeStruct((m, n), out_dtype),
        grid=(n // tn, m // tm),
        in_specs=in_specs,
        out_specs=pl.BlockSpec((tm, tn), lambda j, i: (i, j)),
        scratch_shapes=[pltpu.VMEM((kd, tn), BF16)],
        compiler_params=_cparams(("parallel", "arbitrary")),
        name=name,
    )(*args)


def _ep_rmsnorm(acc, g_ref):
    return _rms(acc, g_ref[...])


def _ep_sigmoid_bias(acc, b_ref):
    return _sigmoid(acc + b_ref[...])


def _ep_residual(acc, r_ref):
    return r_ref[...] + acc


def _mm_tout_kernel(x_ref, w_ref, o_ref, *, t, wt, scale):
    if wt:
        res = _dot_nt(w_ref[...], x_ref[...]) * scale
        for c in range(o_ref.shape[0]):
            o_ref[c] = res[:, c * t:(c + 1) * t].astype(o_ref.dtype)
        return
    res = jnp.dot(x_ref[...], w_ref[...], preferred_element_type=F32) * scale
    for c in range(o_ref.shape[0]):
        o_ref[c] = res[c * t:(c + 1) * t, :].T.astype(o_ref.dtype)


def matmul_tout(x, w, t, name, *, col0=0, n_out=None, wt=False, scale=1.0):
    m, kd = x.shape
    n = w.shape[0 if wt else 1] - col0 if n_out is None else n_out
    tm, tn = _pick(m, 1024), _pick(math.gcd(n, col0), 512)
    cb = col0 // tn
    return pl.pallas_call(
        functools.partial(_mm_tout_kernel, t=t, wt=wt, scale=scale),
        out_shape=jax.ShapeDtypeStruct((m // t, n, t), BF16),
        grid=(m // tm, n // tn),
        in_specs=[pl.BlockSpec((tm, kd), lambda i, j: (i, 0)),
                  pl.BlockSpec((tn, kd), lambda i, j: (cb + j, 0)) if wt else
                  pl.BlockSpec((kd, tn), lambda i, j: (0, cb + j))],
        out_specs=pl.BlockSpec((tm // t, tn, t), lambda i, j: (i, j, 0)),
        compiler_params=_cparams(("parallel", "parallel")),
        name=name,
    )(x, w)


def _window_cast_kernel(*refs, r):
    if r == 0:
        a_ref, o_ref = refs
        o_ref[...] = a_ref[0].astype(o_ref.dtype)
    else:
        a_ref, b_ref, o_ref = refs
        o_ref[...] = jnp.concatenate([a_ref[0, r:, :], b_ref[0, :r, :]], axis=0).astype(o_ref.dtype)


def window_cast(wt3, layer, off, width, name):
    _, _, d = wt3.shape
    tr = _pick(width, 256)
    start = (off // tr) * tr
    r = off - start
    if r % 8:
        return wt3[layer, off:off + width, :].astype(BF16)
    in_specs = [pl.BlockSpec((1, tr, d), lambda j: (layer, start // tr + j, 0))]
    if r:
        rb = LANES if r <= LANES else tr
        in_specs.append(pl.BlockSpec((1, rb, d), lambda j: (layer, (start + (j + 1) * tr) // rb, 0)))
    return pl.pallas_call(
        functools.partial(_window_cast_kernel, r=r),
        out_shape=jax.ShapeDtypeStruct((width, d), BF16),
        grid=(width // tr,),
        in_specs=in_specs,
        out_specs=pl.BlockSpec((tr, d), lambda j: (j, 0)),
        compiler_params=_cparams(("parallel",)),
        name=name,
    )(*([wt3] * len(in_specs)))


def _small_kernel(h_ref, w_ref, g_ref, bf_ref, cos_ref, sin_ref,
                  kvn_ref, kr_ref, lf_ref, *, kvl):
    acc = _dot_nt(h_ref[...], w_ref[...])
    kvn_ref[...] = _rms(acc[:, :kvl], g_ref[...]).astype(kvn_ref.dtype)
    r = acc[:, kvl:kvl + LANES]
    p = acc[:, kvl + LANES:kvl + 2 * LANES]
    kr_ref[...] = (r * cos_ref[...] + p * sin_ref[...]).astype(kr_ref.dtype)
    lf_ref[...] = _log_sigmoid(acc[:, kvl + 2 * LANES:kvl + 3 * LANES] + bf_ref[...])


def inproj_small(h, w, kv_norm, b_forget_pad, cos_t, sin_t, seq, name):
    m, d = h.shape
    kvl = kv_norm.shape[0]
    n = w.shape[0]
    tm = _pick(seq, 512)
    nseq = seq // tm
    return pl.pallas_call(
        functools.partial(_small_kernel, kvl=kvl),
        out_shape=(jax.ShapeDtypeStruct((m, kvl), BF16),
                   jax.ShapeDtypeStruct((m, LANES), BF16),
                   jax.ShapeDtypeStruct((m, LANES), F32)),
        grid=(m // tm,),
        in_specs=[pl.BlockSpec((tm, d), lambda i: (i, 0)),
                  pl.BlockSpec((n, d), lambda i: (0, 0)),
                  pl.BlockSpec((1, kvl), lambda i: (0, 0)),
                  pl.BlockSpec((1, LANES), lambda i: (0, 0)),
                  pl.BlockSpec((tm, LANES), lambda i: (i % nseq, 0)),
                  pl.BlockSpec((tm, LANES), lambda i: (i % nseq, 0))],
        out_specs=(pl.BlockSpec((tm, kvl), lambda i: (i, 0)),
                   pl.BlockSpec((tm, LANES), lambda i: (i, 0)),
                   pl.BlockSpec((tm, LANES), lambda i: (i, 0))),
        compiler_params=_cparams(("parallel",)),
        name=name,
    )(h, w, kv_norm.reshape(1, kvl), b_forget_pad, cos_t, sin_t)


def _mlaq_kernel(x_ref, wt_ref, cos_ref, sin_ref, o_ref, *, heads, qscale):
    hw = heads * HEAD_DIM
    x = x_ref[...]
    nope = _dot_nt(wt_ref[:hw, :], x)
    o_ref[0, :hw, :] = (nope * qscale).astype(o_ref.dtype)
    r = _dot_nt(wt_ref[hw:2 * hw, :], x)
    p = _dot_nt(wt_ref[2 * hw:, :], x)
    cos_t, sin_t = cos_ref[...] * qscale, sin_ref[...] * qscale
    for h in range(heads):
        sl = slice(h * HEAD_DIM, (h + 1) * HEAD_DIM)
        o_ref[0, hw + h * HEAD_DIM:hw + (h + 1) * HEAD_DIM, :] = (
            r[sl, :] * cos_t + p[sl, :] * sin_t).astype(o_ref.dtype)


def mla_q(qn, wt, cos_tt, sin_tt, seq, heads, qscale, t, name):
    m, ql = qn.shape
    n = wt.shape[0]
    nseq = seq // t
    return pl.pallas_call(
        functools.partial(_mlaq_kernel, heads=heads, qscale=qscale),
        out_shape=jax.ShapeDtypeStruct((m // t, 2 * heads * HEAD_DIM, t), BF16),
        grid=(m // t,),
        in_specs=[pl.BlockSpec((t, ql), lambda i: (i, 0)),
                  pl.BlockSpec((n, ql), lambda i: (0, 0)),
                  pl.BlockSpec((LANES, t), lambda i: (0, i % nseq)),
                  pl.BlockSpec((LANES, t), lambda i: (0, i % nseq))],
        out_specs=pl.BlockSpec((1, 2 * heads * HEAD_DIM, t), lambda i: (i, 0, 0)),
        compiler_params=_cparams(("parallel",)),
        name=name,
    )(qn, wt, cos_tt, sin_tt)


def _split(x, parts):
    out = []
    for _ in range(parts - 1):
        hi = x.astype(BF16)
        out.append(hi)
        x = x - hi.astype(F32)
    out.append(x.astype(BF16))
    return out


def _cumsum_kernel(x_ref, o_ref, *, seq, tc):
    rows = lax.broadcasted_iota(jnp.int32, (tc, tc), 0)
    cols = lax.broadcasted_iota(jnp.int32, (tc, tc), 1)
    tri = jnp.where(cols <= rows, 1.0, 0.0).astype(BF16)

    def body(c, carry):
        r0 = pl.multiple_of(c * tc, tc)
        x = x_ref[0, pl.ds(r0, tc), :]
        acc = jnp.zeros((tc, LANES), F32)
        for part in _split(x, 3):
            acc = acc + jnp.dot(tri, part, preferred_element_type=F32)
        acc = acc + carry
        o_ref[0, pl.ds(r0, tc), :] = acc * LOG2E
        return acc[tc - 1:tc, :]

    lax.fori_loop(0, seq // tc, body, jnp.zeros((1, LANES), F32))


def cumsum_rows(x, name):
    b, seq, n = x.shape
    tc = _pick(seq, 256)
    return pl.pallas_call(
        functools.partial(_cumsum_kernel, seq=seq, tc=tc),
        out_shape=jax.ShapeDtypeStruct((b, seq, n), F32),
        grid=(b,),
        in_specs=[pl.BlockSpec((1, seq, n), lambda i: (i, 0, 0))],
        out_specs=pl.BlockSpec((1, seq, n), lambda i: (i, 0, 0)),
        compiler_params=_cparams(("parallel",)),
        name=name,
    )(x)


def _head(ref, i, rows=slice(None)):
    return ref[rows, i * HEAD_DIM:(i + 1) * HEAD_DIM]


def _head_t(ref, i):
    return ref[0, i * HEAD_DIM:(i + 1) * HEAD_DIM, :]


def _pipeline(n, chunk_of, scores_into, consume, buf_a, buf_b, mask_last):
    scores_into(buf_a, chunk_of(0))

    def pair(p, _):
        scores_into(buf_b, chunk_of(2 * p + 1))
        consume(buf_a, chunk_of(2 * p), False)
        scores_into(buf_a, chunk_of(2 * p + 2))
        consume(buf_b, chunk_of(2 * p + 1), False)
        return 0

    lax.fori_loop(0, (n - 1) // 2, pair, 0)

    @pl.when(n % 2 == 1)
    def _():
        consume(buf_a, chunk_of(n - 1), mask_last)

    @pl.when(n % 2 == 0)
    def _():
        scores_into(buf_b, chunk_of(n - 1))
        consume(buf_a, chunk_of(n - 2), False)
        consume(buf_b, chunk_of(n - 1), mask_last)


def _softmax_consume(buf, vts, mask, m_ref, l_ref, acc_ref):
    g = len(vts)
    ps, alphas = [], []
    for i in range(g):
        s = buf[i] if mask is None else jnp.where(mask, buf[i], NEG_BIG)
        m_old = m_ref[i]
        m_new = jnp.maximum(m_old, jnp.max(s, axis=0, keepdims=True))
        alpha = jnp.exp2(m_old - m_new)
        p = jnp.exp2(s - m_new)
        m_ref[i] = m_new
        l_ref[i] = alpha * l_ref[i] + jnp.sum(p, axis=0, keepdims=True)
        alphas.append(alpha)
        ps.append(p.astype(BF16))
    pvs = [jnp.dot(vt, p, preferred_element_type=F32) for vt, p in zip(vts, ps)]
    for i in range(g):
        acc_ref[i] = alphas[i] * acc_ref[i] + pvs[i]


def _softmax_init(m_ref, l_ref, acc_ref):
    m_ref[...] = jnp.full(m_ref.shape, NEG_BIG, F32)
    l_ref[...] = jnp.zeros(l_ref.shape, F32)
    acc_ref[...] = jnp.zeros(acc_ref.shape, F32)


def _softmax_store(o_ref, l_ref, acc_ref):
    for i in range(acc_ref.shape[0]):
        o_ref[:, i * HEAD_DIM:(i + 1) * HEAD_DIM] = (
            acc_ref[i] * (1.0 / l_ref[i])).T.astype(o_ref.dtype)


def _attn_scratch(g, t):
    return [pltpu.VMEM((g, t, t), F32), pltpu.VMEM((g, t, t), F32),
            pltpu.VMEM((g, 1, t), F32), pltpu.VMEM((g, 1, t), F32),
            pltpu.VMEM((g, HEAD_DIM, t), F32)]


def _mla_kernel(qn_ref, qr_ref, kn_ref, kr_ref, vt_ref, o_ref,
                sa_ref, sb_ref, m_ref, l_ref, acc_ref, *, t, g):
    qi = pl.program_id(2)
    qts = [jnp.concatenate([_head_t(qn_ref, i), _head_t(qr_ref, i)], axis=0) for i in range(g)]
    keys = lax.broadcasted_iota(jnp.int32, (t, t), 0) // CHUNK
    queries = lax.broadcasted_iota(jnp.int32, (t, t), 1) // CHUNK
    mask = keys <= queries

    def scores_into(buf, j):
        rows = pl.ds(pl.multiple_of(j * t, t), t)
        kr = kr_ref[rows, :]
        for i in range(g):
            buf[i] = jnp.dot(jnp.concatenate([_head(kn_ref, i, rows), kr], axis=-1), qts[i],
                             preferred_element_type=F32)

    def consume(buf, j, masked):
        vts = [vt_ref[j, i * HEAD_DIM:(i + 1) * HEAD_DIM, :] for i in range(g)]
        _softmax_consume(buf, vts, mask if masked else None, m_ref, l_ref, acc_ref)

    _softmax_init(m_ref, l_ref, acc_ref)
    _pipeline(qi + 1, lambda n: n, scores_into, consume, sa_ref, sb_ref, True)
    _softmax_store(o_ref, l_ref, acc_ref)


def mla_attention(q, kn, kr, vt, batch, seq, heads, name):
    m = kn.shape[0]
    t = vt.shape[2]
    nq = seq // t
    g = _head_group(heads)
    gw, hb = g * HEAD_DIM, heads // g
    return pl.pallas_call(
        functools.partial(_mla_kernel, t=t, g=g),
        out_shape=jax.ShapeDtypeStruct((m, heads * HEAD_DIM), BF16),
        grid=(batch, hb, nq),
        in_specs=[pl.BlockSpec((1, gw, t), lambda b, h, i: (b * nq + i, h, 0)),
                  pl.BlockSpec((1, gw, t), lambda b, h, i: (b * nq + i, hb + h, 0)),
                  pl.BlockSpec((seq, gw), lambda b, h, i: (b, h)),
                  pl.BlockSpec((seq, LANES), lambda b, h, i: (b, 0)),
                  pl.BlockSpec((nq, gw, t), lambda b, h, i: (b, h, 0))],
        out_specs=pl.BlockSpec((t, gw), lambda b, h, i: (b * nq + i, h)),
        scratch_shapes=_attn_scratch(g, t),
        compiler_params=_cparams(("parallel", "parallel", "arbitrary")),
        name=name,
    )(q, q, kn, kr, vt)


def _fox_kernel(qt_ref, k_ref, vt_ref, cq_ref, ck_ref, o_ref,
                sa_ref, sb_ref, m_ref, l_ref, acc_ref, *, t, g):
    qi = pl.program_id(2)
    qts = [_head_t(qt_ref, i) for i in range(g)]
    cqs = [cq_ref[i] for i in range(g)]
    keys = lax.broadcasted_iota(jnp.int32, (t, t), 0)
    queries = lax.broadcasted_iota(jnp.int32, (t, t), 1)
    mask = keys <= queries

    def scores_into(buf, j):
        rows = pl.ds(pl.multiple_of(j * t, t), t)
        for i in range(g):
            ck = ck_ref[i, rows, :]
            ck = jnp.concatenate([ck] * (t // LANES), axis=1)
            buf[i] = jnp.dot(_head(k_ref, i, rows), qts[i],
                             preferred_element_type=F32) + cqs[i] - ck

    def consume(buf, j, masked):
        vts = [vt_ref[j, i * HEAD_DIM:(i + 1) * HEAD_DIM, :] for i in range(g)]
        _softmax_consume(buf, vts, mask if masked else None, m_ref, l_ref, acc_ref)

    _softmax_init(m_ref, l_ref, acc_ref)
    _pipeline(qi + 1, lambda n: n, scores_into, consume, sa_ref, sb_ref, True)
    _softmax_store(o_ref, l_ref, acc_ref)


def fox_attention(qt, k, vt, c_row, c_rep, batch, seq, heads, name):
    m = k.shape[0]
    t = vt.shape[2]
    nq = seq // t
    g = _head_group(heads)
    gw, hb = g * HEAD_DIM, heads // g
    return pl.pallas_call(
        functools.partial(_fox_kernel, t=t, g=g),
        out_shape=jax.ShapeDtypeStruct((m, heads * HEAD_DIM), BF16),
        grid=(batch, hb, nq),
        in_specs=[pl.BlockSpec((1, gw, t), lambda b, h, i: (b * nq + i, h, 0)),
                  pl.BlockSpec((seq, gw), lambda b, h, i: (b, h)),
                  pl.BlockSpec((nq, gw, t), lambda b, h, i: (b, h, 0)),
                  pl.BlockSpec((g, 1, t), lambda b, h, i: (b * hb + h, 0, i)),
                  pl.BlockSpec((g, seq, LANES), lambda b, h, i: (b * hb + h, 0, 0))],
        out_specs=pl.BlockSpec((t, gw), lambda b, h, i: (b * nq + i, h)),
        scratch_shapes=_attn_scratch(g, t),
        compiler_params=_cparams(("parallel", "parallel", "arbitrary")),
        name=name,
    )(qt, k, vt, c_row, c_rep)


def _sb_kernel(qt_ref, k_ref, vt_ref, o_ref, za_ref, zb_ref, rest_ref, acc_ref, *, t, g):
    qi = pl.program_id(2)
    qts = [_head_t(qt_ref, i) for i in range(g)]
    keys = lax.broadcasted_iota(jnp.int32, (t, t), 0)
    queries = lax.broadcasted_iota(jnp.int32, (t, t), 1)
    upper = jnp.where(queries >= keys, 1.0, 0.0).astype(BF16)
    strict = keys < queries

    def scores_into(buf, j):
        rows = pl.ds(pl.multiple_of(j * t, t), t)
        for i in range(g):
            buf[i] = jnp.dot(_head(k_ref, i, rows), qts[i],
                             preferred_element_type=F32)

    def consume(buf, j, masked):
        loms = []
        for i in range(g):
            z = buf[i]
            lom = jnp.minimum(-z, 0.0) - jnp.log(1.0 + jnp.exp2(-jnp.abs(z))) * LOG2E
            loms.append(jnp.where(strict, lom, 0.0) if masked else lom)
        incs = [sum(jnp.dot(upper, part, preferred_element_type=F32) for part in _split(lom, 2))
                for lom in loms]
        avs = []
        for i in range(g):
            a = jnp.exp2(buf[i] + incs[i] + rest_ref[i])
            avs.append((jnp.where(strict, a, 0.0) if masked else a).astype(BF16))
            rest_ref[i] = rest_ref[i] + incs[i][0:1, :]
        pvs = [jnp.dot(vt_ref[j, i * HEAD_DIM:(i + 1) * HEAD_DIM, :], avs[i],
                       preferred_element_type=F32) for i in range(g)]
        for i in range(g):
            acc_ref[i] = acc_ref[i] + pvs[i]

    rest_ref[...] = jnp.zeros(rest_ref.shape, F32)
    acc_ref[...] = jnp.zeros(acc_ref.shape, F32)
    scores_into(za_ref, qi)
    consume(za_ref, qi, True)

    @pl.when(qi > 0)
    def _():
        _pipeline(qi, lambda n: qi - 1 - n, scores_into, consume, za_ref, zb_ref, False)

    for i in range(g):
        o_ref[:, i * HEAD_DIM:(i + 1) * HEAD_DIM] = acc_ref[i].T.astype(o_ref.dtype)


def sb_attention(qt, k, vt, batch, seq, heads, name):
    m = k.shape[0]
    t = vt.shape[2]
    nq = seq // t
    g = _head_group(heads)
    gw, hb = g * HEAD_DIM, heads // g
    return pl.pallas_call(
        functools.partial(_sb_kernel, t=t, g=g),
        out_shape=jax.ShapeDtypeStruct((m, heads * HEAD_DIM), BF16),
        grid=(batch, hb, nq),
        in_specs=[pl.BlockSpec((1, gw, t), lambda b, h, i: (b * nq + i, h, 0)),
                  pl.BlockSpec((seq, gw), lambda b, h, i: (b, h)),
                  pl.BlockSpec((nq, gw, t), lambda b, h, i: (b, h, 0))],
        out_specs=pl.BlockSpec((t, gw), lambda b, h, i: (b * nq + i, h)),
        scratch_shapes=[pltpu.VMEM((g, t, t), F32), pltpu.VMEM((g, t, t), F32),
                        pltpu.VMEM((g, 1, t), F32), pltpu.VMEM((g, HEAD_DIM, t), F32)],
        compiler_params=_cparams(("parallel", "parallel", "arbitrary")),
        name=name,
    )(qt, k, vt)


def _merge_kernel(oa_ref, ob_ref, oc_ref, wa_ref, wb_ref, wc_ref,
                  ga_ref, gb_ref, gc_ref, o_ref, wa_s, wb_s, wc_s):
    @pl.when(pl.program_id(1) == 0)
    def _():
        wa_s[...] = wa_ref[...].astype(BF16)
        wb_s[...] = wb_ref[...].astype(BF16)
        wc_s[...] = wc_ref[...].astype(BF16)

    acc = ga_ref[...].astype(F32) * jnp.dot(oa_ref[...], wa_s[...], preferred_element_type=F32)
    acc += gb_ref[...].astype(F32) * jnp.dot(ob_ref[...], wb_s[...], preferred_element_type=F32)
    acc += gc_ref[...].astype(F32) * jnp.dot(oc_ref[...], wc_s[...], preferred_element_type=F32)
    o_ref[...] = acc.astype(o_ref.dtype)


def gated_merge(oa, ob, oc, wa, wb, wc, gates, name):
    m = oa.shape[0]
    d = wa.shape[1]
    tm, tn = _pick(m, 1024), _pick(d, 512)
    nj = d // tn
    row = lambda a: pl.BlockSpec((tm, a.shape[1]), lambda j, i: (i, 0))
    col = lambda a: pl.BlockSpec((a.shape[0], tn), lambda j, i: (0, j))
    gate = lambda g: pl.BlockSpec((tm, tn), lambda j, i: (i, g * nj + j))
    return pl.pallas_call(
        _merge_kernel,
        out_shape=jax.ShapeDtypeStruct((m, d), BF16),
        grid=(nj, m // tm),
        in_specs=[row(oa), row(ob), row(oc), col(wa), col(wb), col(wc),
                  gate(0), gate(1), gate(2)],
        out_specs=pl.BlockSpec((tm, tn), lambda j, i: (i, j)),
        scratch_shapes=[pltpu.VMEM((a.shape[0], tn), BF16) for a in (wa, wb, wc)],
        compiler_params=_cparams(("parallel", "arbitrary")),
        name=name,
    )(oa, ob, oc, wa, wb, wc, gates, gates, gates)


def _ffn_up_kernel(h_ref, wg_ref, wu_ref, cw_ref, cb_ref, o_ref, carry_ref, wg_s, wu_s,
                   *, tiles_per_seq):
    i = pl.program_id(1)

    @pl.when(i == 0)
    def _():
        wg_s[...] = wg_ref[...].astype(BF16)
        wu_s[...] = wu_ref[...].astype(BF16)

    g = jnp.dot(h_ref[...], wg_s[...], preferred_element_type=F32)
    u = jnp.dot(h_ref[...], wu_s[...], preferred_element_type=F32)
    tm = g.shape[0]

    @pl.when(i % tiles_per_seq == 0)
    def _():
        carry_ref[...] = jnp.zeros_like(carry_ref)

    prev = carry_ref[...]
    carry_ref[...] = g[tm - 8:, :]
    row = lax.broadcasted_iota(jnp.int32, g.shape, 0)
    g1 = jnp.where(row == 0, prev[7:8, :], pltpu.roll(g, 1, 0))
    g2 = jnp.where(row == 0, prev[6:7, :],
                   jnp.where(row == 1, prev[7:8, :], pltpu.roll(g, 2, 0)))
    cw = cw_ref[...]
    conv = cb_ref[...] + cw[0:1, :] * g2 + cw[1:2, :] * g1 + cw[2:3, :] * g
    act = conv * (1.0 / (1.0 + jnp.exp(-conv)))
    o_ref[...] = (act * u).astype(o_ref.dtype)


def ffn_up(h, wg, wu, conv_w, conv_b, seq, name):
    m, d = h.shape
    f = wg.shape[1]
    tm, tn = _pick(seq, 1024), _pick(f, 256)
    return pl.pallas_call(
        functools.partial(_ffn_up_kernel, tiles_per_seq=seq // tm),
        out_shape=jax.ShapeDtypeStruct((m, f), BF16),
        grid=(f // tn, m // tm),
        in_specs=[pl.BlockSpec((tm, d), lambda j, i: (i, 0)),
                  pl.BlockSpec((d, tn), lambda j, i: (0, j)),
                  pl.BlockSpec((d, tn), lambda j, i: (0, j)),
                  pl.BlockSpec((CONV_WIDTH, tn), lambda j, i: (0, j)),
                  pl.BlockSpec((1, tn), lambda j, i: (0, j))],
        out_specs=pl.BlockSpec((tm, tn), lambda j, i: (i, j)),
        scratch_shapes=[pltpu.VMEM((8, tn), F32), pltpu.VMEM((d, tn), BF16),
                        pltpu.VMEM((d, tn), BF16)],
        compiler_params=_cparams(("arbitrary", "arbitrary")),
        name=name,
    )(h, wg, wu, conv_w, conv_b.reshape(1, f))


def _rope_tables(seq):
    inv = 1.0 / (ROPE_THETA ** (jnp.arange(0, ROPE, 2, dtype=F32) / ROPE))
    ang = jnp.arange(seq, dtype=F32)[:, None] * inv[None, :]
    zeros = jnp.zeros((seq, LANES - ROPE), F32)
    cos_t = jnp.concatenate([jnp.cos(ang), jnp.cos(ang), zeros], axis=-1)
    sin_t = jnp.concatenate([jnp.sin(ang), jnp.sin(ang), zeros], axis=-1)
    return cos_t, sin_t


def _rope_partner(w):
    half = ROPE // 2
    return jnp.concatenate([-w[..., half:], w[..., :half]], axis=-1)


def _in_offsets(d):
    ql, kvl = d // 4, d // 8
    hf = d // (4 * HEAD_DIM)
    fw = hf * HEAD_DIM
    o_kv = ql
    o_kr = o_kv + kvl
    o_f = o_kr + ROPE
    o_fpre = o_f + 3 * fw
    o_s = o_fpre + hf
    o_g = o_s + 3 * fw
    return dict(kv=o_kv, kr=o_kr, f=o_f, fpre=o_fpre, s=o_s, g=o_g)


def _small_weight(win_kv, win_kr, win_fp, hf):
    d = win_kv.shape[1]
    zpad = jnp.zeros((LANES - ROPE, d), BF16)
    kr = win_kr[:ROPE]
    half = ROPE // 2
    partner = jnp.concatenate([-kr[half:], kr[:half]], axis=0)
    return jnp.concatenate([win_kv, kr, zpad, partner, zpad,
                            win_fp[:hf], jnp.zeros((LANES - hf, d), BF16)], axis=0)


def _layer_weights(d, w_uq, w_ukv):
    ql, kvl = d // 4, d // 8
    hm = d // (2 * HEAD_DIM)
    uq = w_uq.reshape(ql, hm, HEAD_DIM + ROPE)
    r = uq[:, :, HEAD_DIM:]
    z = jnp.zeros((ql, hm, LANES - ROPE), F32)
    w_q = jnp.concatenate(
        [uq[:, :, :HEAD_DIM].reshape(ql, hm * HEAD_DIM),
         jnp.concatenate([r, z], axis=-1).reshape(ql, hm * HEAD_DIM),
         jnp.concatenate([_rope_partner(r), z], axis=-1).reshape(ql, hm * HEAD_DIM)],
        axis=-1).T.astype(BF16)
    ukv = w_ukv.reshape(kvl, hm, 2 * HEAD_DIM)
    w_kv = jnp.concatenate([ukv[:, :, :HEAD_DIM].reshape(kvl, hm * HEAD_DIM),
                            ukv[:, :, HEAD_DIM:].reshape(kvl, hm * HEAD_DIM)], axis=-1).astype(BF16)
    return dict(uq=w_q, ukv=w_kv)


def kernel(x, attn_norm, w_in, b_forget, b_gate, q_norm, w_uq, kv_norm, w_ukv, w_br_mla, w_br_fox, w_br_sb, w_o, ffn_norm, w_ffn_gate, conv_w, conv_b, w_ffn_up, w_ffn_down, final_norm):
    batch, seq, d = x.shape
    depth = w_in.shape[0]
    m = batch * seq
    hm, hf = d // (2 * HEAD_DIM), d // (4 * HEAD_DIM)
    hs = hf
    fw = hf * HEAD_DIM
    t_att = _pick(seq, ATT_TILE)
    cos_t, sin_t = _rope_tables(seq)
    cos_tt, sin_tt = cos_t.T, sin_t.T
    xf = x.reshape(m, d)
    q_scale = LOG2E / math.sqrt(HEAD_DIM)
    mla_scale = LOG2E / math.sqrt(HEAD_DIM + ROPE)
    off = _in_offsets(d)

    w_in_t = jnp.swapaxes(w_in, 1, 2)

    for l in range(depth):
        w = _layer_weights(d, w_uq[l], w_ukv[l])
        w_small = _small_weight(window_cast(w_in_t, l, off["kv"], d // 8, f"w_kvlat_{l}"),
                                window_cast(w_in_t, l, off["kr"], LANES, f"w_krope_{l}"),
                                window_cast(w_in_t, l, off["fpre"], LANES, f"w_fpre_{l}"), hf)
        w_ql = window_cast(w_in_t, l, 0, d // 4, f"w_qlat_{l}")
        w_f = window_cast(w_in_t, l, off["f"], 3 * fw, f"w_fox_{l}")
        w_s = window_cast(w_in_t, l, off["s"], 3 * fw, f"w_sb_{l}")
        w_g = window_cast(w_in_t, l, off["g"], 3 * d, f"w_gate_{l}")
        h = rmsnorm(xf, attn_norm[l], BF16, f"attn_norm_{l}")

        qn = matmul(h, w_ql, out_dtype=BF16, name=f"inproj_qlat_{l}", wt=True,
                    epilogue=_ep_rmsnorm, extras=[(q_norm[l].reshape(1, -1), "row")],
                    tn=w_ql.shape[0])
        bf_pad = jnp.zeros((1, LANES), F32).at[0, :hf].set(b_forget[l])
        kvn, kr, logf = inproj_small(h, w_small, kv_norm[l], bf_pad, cos_t, sin_t, seq,
                                     f"inproj_small_{l}")
        qt_f = matmul_tout(h, w_f, t_att, f"inproj_fqt_{l}", n_out=fw, wt=True, scale=q_scale)
        k_f = matmul(h, w_f, out_dtype=BF16, name=f"inproj_fk_{l}", col0=fw, n_out=fw, wt=True,
                     tn=1024)
        vt_f = matmul_tout(h, w_f, t_att, f"inproj_fvt_{l}", col0=2 * fw, wt=True)
        qt_s = matmul_tout(h, w_s, t_att, f"inproj_sqt_{l}", n_out=fw, wt=True, scale=q_scale)
        k_s = matmul(h, w_s, out_dtype=BF16, name=f"inproj_sk_{l}", col0=fw, n_out=fw, wt=True,
                     tn=1024)
        vt_s = matmul_tout(h, w_s, t_att, f"inproj_svt_{l}", col0=2 * fw, wt=True)
        gates = matmul(h, w_g, out_dtype=BF16, name=f"inproj_gate_{l}", wt=True,
                       epilogue=_ep_sigmoid_bias, extras=[(b_gate[l].reshape(1, -1), "row")],
                       tn=1024)

        q = mla_q(qn, w["uq"], cos_tt, sin_tt, seq, hm, mla_scale, t_att, f"mla_q_{l}")
        kn = matmul(kvn, w["ukv"], out_dtype=BF16, name=f"mla_k_{l}", n_out=hm * HEAD_DIM)
        vt_mla = matmul_tout(kvn, w["ukv"], t_att, f"mla_vt_{l}", col0=hm * HEAD_DIM)
        o_mla = mla_attention(q, kn, kr, vt_mla, batch, seq, hm, f"mla_attn_{l}")

        c = cumsum_rows(logf.reshape(batch, seq, LANES), f"fox_cumsum_{l}")
        c_heads = jnp.transpose(c[:, :, :hf], (0, 2, 1)).reshape(batch * hf, seq)
        c_rep = jnp.broadcast_to(c_heads[:, :, None], (batch * hf, seq, LANES))
        o_fox = fox_attention(qt_f, k_f, vt_f, c_heads.reshape(batch * hf, 1, seq), c_rep,
                              batch, seq, hf, f"fox_attn_{l}")

        o_sb = sb_attention(qt_s, k_s, vt_s, batch, seq, hs, f"sb_attn_{l}")

        merged = gated_merge(o_mla, o_fox, o_sb, w_br_mla[l], w_br_fox[l], w_br_sb[l], gates,
                             f"merge_{l}")
        xf = matmul_ws(merged, w_o[l], out_dtype=F32, name=f"out_proj_{l}",
                       epilogue=_ep_residual, extras=[(xf, "tile")])

        h = rmsnorm(xf, ffn_norm[l], BF16, f"ffn_norm_{l}")
        act = ffn_up(h, w_ffn_gate[l], w_ffn_up[l], conv_w[l], conv_b[l], seq, f"ffn_up_{l}")
        xf = matmul(act, w_ffn_down[l].astype(BF16), out_dtype=F32, name=f"ffn_down_{l}",
                    epilogue=_ep_residual, extras=[(xf, "tile")], tk=4096)

    return rmsnorm(xf, final_norm, F32, "final_norm").reshape(batch, seq, d)
```

```python
import functools
import math

import jax
import jax.numpy as jnp
from jax import lax
from jax.experimental import pallas as pl
from jax.experimental.pallas import tpu as pltpu

HEAD_DIM = 128
CHUNK = 64
ROPE = 64
ROPE_THETA = 10000.0
CONV_WIDTH = 3
EPS = 1e-6
LANES = 128
NEG_BIG = -1e30
LOG2E = math.log2(math.e)
VMEM_LIMIT = 56 * 1024 * 1024
ATT_TILE = 256
ATT_HEADS = 4

F32 = jnp.float32
BF16 = jnp.bfloat16


def _cparams(sem):
    return pltpu.CompilerParams(dimension_semantics=sem, vmem_limit_bytes=VMEM_LIMIT)


def _pick(n, pref):
    t = min(n, pref)
    while n % t:
        t //= 2
    return t


def _head_group(heads):
    g = ATT_HEADS
    while heads % g:
        g //= 2
    return g


def _rms(x, g):
    return x * lax.rsqrt(jnp.mean(x * x, axis=-1, keepdims=True) + EPS) * g


def _sigmoid(x):
    return 1.0 / (1.0 + jnp.exp(-x))


def _log_sigmoid(x):
    return jnp.minimum(x, 0.0) - jnp.log(1.0 + jnp.exp(-jnp.abs(x)))


def _rmsnorm_kernel(x_ref, g_ref, o_ref):
    o_ref[...] = _rms(x_ref[...], g_ref[...]).astype(o_ref.dtype)


def rmsnorm(x, g, out_dtype, name):
    m, d = x.shape
    tm = _pick(m, 256)
    return pl.pallas_call(
        _rmsnorm_kernel,
        out_shape=jax.ShapeDtypeStruct((m, d), out_dtype),
        grid=(m // tm,),
        in_specs=[pl.BlockSpec((tm, d), lambda i: (i, 0)),
                  pl.BlockSpec((1, d), lambda i: (0, 0))],
        out_specs=pl.BlockSpec((tm, d), lambda i: (i, 0)),
        compiler_params=_cparams(("parallel",)),
        name=name,
    )(x, g.reshape(1, d))


def _dot_nt(a, b):
    return lax.dot_general(a, b, (((1,), (1,)), ((), ())), preferred_element_type=F32)


def _wblock(ref):
    return ref[0] if len(ref.shape) == 3 else ref[...]


def _wspec(w, layer, shape, index):
    if layer is None:
        return pl.BlockSpec(shape, index)
    return pl.BlockSpec((1,) + shape, lambda *g: (layer,) + index(*g))


def _mm_kernel(*refs, nk, n_extra, epilogue, wt):
    x_ref, w_ref = refs[0], refs[1]
    extras = refs[2:2 + n_extra]
    o_ref = refs[2 + n_extra]
    if nk == 1:
        if wt:
            acc = _dot_nt(x_ref[...], _wblock(w_ref))
        else:
            acc = jnp.dot(x_ref[...], _wblock(w_ref), preferred_element_type=F32)
        o_ref[...] = epilogue(acc, *extras).astype(o_ref.dtype)
        return
    acc_ref = refs[3 + n_extra]
    k = pl.program_id(2)

    @pl.when(k == 0)
    def _():
        acc_ref[...] = jnp.dot(x_ref[...], _wblock(w_ref), preferred_element_type=F32)

    @pl.when((k > 0) & (k < nk - 1))
    def _():
        acc_ref[...] += jnp.dot(x_ref[...], _wblock(w_ref), preferred_element_type=F32)

    @pl.when(k == nk - 1)
    def _():
        acc = acc_ref[...] + jnp.dot(x_ref[...], _wblock(w_ref), preferred_element_type=F32)
        o_ref[...] = epilogue(acc, *extras).astype(o_ref.dtype)


def matmul(x, w, *, out_dtype, name, epilogue=None, extras=(), tm=1024, tn=512, tk=None,
           col0=0, n_out=None, wt=False, layer=None):
    m, kd = x.shape
    n = w.shape[-2 if wt else -1] - col0 if n_out is None else n_out
    assert not (wt and tk is not None)
    tm, tn = _pick(m, tm), _pick(math.gcd(n, col0), tn)
    cb = col0 // tn
    tk = kd if tk is None else _pick(kd, tk)
    nk = kd // tk
    if epilogue is None:
        epilogue = lambda acc: acc
    in_specs = [pl.BlockSpec((tm, tk), lambda i, j, k: (i, k)),
                _wspec(w, layer, (tn, tk), lambda i, j, k: (cb + j, k)) if wt else
                _wspec(w, layer, (tk, tn), lambda i, j, k: (k, cb + j))]
    args = [x, w]
    for arr, kind in extras:
        if kind == "row":
            in_specs.append(pl.BlockSpec((1, tn), lambda i, j, k: (0, j)))
        else:
            in_specs.append(pl.BlockSpec((tm, tn), lambda i, j, k: (i, j)))
        args.append(arr)
    scratch = [pltpu.VMEM((tm, tn), F32)] if nk > 1 else []
    return pl.pallas_call(
        functools.partial(_mm_kernel, nk=nk, n_extra=len(extras), epilogue=epilogue, wt=wt),
        out_shape=jax.ShapeDtypeStruct((m, n), out_dtype),
        grid=(m // tm, n // tn, nk),
        in_specs=in_specs,
        out_specs=pl.BlockSpec((tm, tn), lambda i, j, k: (i, j)),
        scratch_shapes=scratch,
        compiler_params=_cparams(("parallel", "parallel", "arbitrary")),
        name=name,
    )(*args)


def _mm_ws_kernel(*refs, n_extra, epilogue):
    x_ref, w_ref = refs[0], refs[1]
    extras = refs[2:2 + n_extra]
    o_ref, w_s = refs[2 + n_extra], refs[3 + n_extra]

    @pl.when(pl.program_id(1) == 0)
    def _():
        w_s[...] = _wblock(w_ref).astype(BF16)

    acc = jnp.dot(x_ref[...], w_s[...], preferred_element_type=F32)
    o_ref[...] = epilogue(acc, *extras).astype(o_ref.dtype)


def matmul_ws(x, w, *, out_dtype, name, epilogue=None, extras=(), tm=1024, tn=512, layer=None):
    m, kd = x.shape
    n = w.shape[-1]
    tm, tn = _pick(m, tm), _pick(n, tn)
    if epilogue is None:
        epilogue = lambda acc: acc
    in_specs = [pl.BlockSpec((tm, kd), lambda j, i: (i, 0)),
                _wspec(w, layer, (kd, tn), lambda j, i: (0, j))]
    args = [x, w]
    for arr, kind in extras:
        if kind == "row":
            in_specs.append(pl.BlockSpec((1, tn), lambda j, i: (0, j)))
        else:
            in_specs.append(pl.BlockSpec((tm, tn), lambda j, i: (i, j)))
        args.append(arr)
    return pl.pallas_call(
        functools.partial(_mm_ws_kernel, n_extra=len(extras), epilogue=epilogue),
        out_shape=jax.ShapeDtypeStruct((m, n), out_dtype),
        grid=(n // tn, m // tm),
        in_specs=in_specs,
        out_specs=pl.BlockSpec((tm, tn), lambda j, i: (i, j)),
        scratch_shapes=[pltpu.VMEM((kd, tn), BF16)],
        compiler_params=_cparams(("parallel", "arbitrary")),
        name=name,
    )(*args)


def _ep_rmsnorm(acc, g_ref):
    return _rms(acc, g_ref[...])


def _ep_sigmoid_bias(acc, b_ref):
    return _sigmoid(acc + b_ref[...])


def _ep_residual(acc, r_ref):
    return r_ref[...] + acc


def _mm_tout_kernel(x_ref, w_ref, o_ref, *, t, wt, scale):
    if wt:
        res = _dot_nt(w_ref[...], x_ref[...]) * scale
        for c in range(o_ref.shape[0]):
            o_ref[c] = res[:, c * t:(c + 1) * t].astype(o_ref.dtype)
        return
    res = jnp.dot(x_ref[...], w_ref[...], preferred_element_type=F32) * scale
    for c in range(o_ref.shape[0]):
        o_ref[c] = res[c * t:(c + 1) * t, :].T.astype(o_ref.dtype)


def matmul_tout(x, w, t, name, *, col0=0, n_out=None, wt=False, scale=1.0):
    m, kd = x.shape
    n = w.shape[0 if wt else 1] - col0 if n_out is None else n_out
    tm, tn = _pick(m, 1024), _pick(math.gcd(n, col0), 512)
    cb = col0 // tn
    return pl.pallas_call(
        functools.partial(_mm_tout_kernel, t=t, wt=wt, scale=scale),
        out_shape=jax.ShapeDtypeStruct((m // t, n, t), BF16),
        grid=(m // tm, n // tn),
        in_specs=[pl.BlockSpec((tm, kd), lambda i, j: (i, 0)),
                  pl.BlockSpec((tn, kd), lambda i, j: (cb + j, 0)) if wt else
                  pl.BlockSpec((kd, tn), lambda i, j: (0, cb + j))],
        out_specs=pl.BlockSpec((tm // t, tn, t), lambda i, j: (i, j, 0)),
        compiler_params=_cparams(("parallel", "parallel")),
        name=name,
    )(x, w)


def _window_cast_kernel(*refs, r):
    if r == 0:
        a_ref, o_ref = refs
        o_ref[...] = a_ref[0].astype(o_ref.dtype)
    else:
        a_ref, b_ref, o_ref = refs
        o_ref[...] = jnp.concatenate([a_ref[0, r:, :], b_ref[0, :r, :]], axis=0).astype(o_ref.dtype)


def window_cast(wt3, layer, off, width, name):
    _, _, d = wt3.shape
    tr = _pick(width, 256)
    start = (off // tr) * tr
    r = off - start
    if r % 8:
        return wt3[layer, off:off + width, :].astype(BF16)
    in_specs = [pl.BlockSpec((1, tr, d), lambda j: (layer, start // tr + j, 0))]
    if r:
        rb = LANES if r <= LANES else tr
        in_specs.append(pl.BlockSpec((1, rb, d), lambda j: (layer, (start + (j + 1) * tr) // rb, 0)))
    return pl.pallas_call(
        functools.partial(_window_cast_kernel, r=r),
        out_shape=jax.ShapeDtypeStruct((width, d), BF16),
        grid=(width // tr,),
        in_specs=in_specs,
        out_specs=pl.BlockSpec((tr, d), lambda j: (j, 0)),
        compiler_params=_cparams(("parallel",)),
        name=name,
    )(*([wt3] * len(in_specs)))


def _small_kernel(h_ref, w_ref, g_ref, bf_ref, cos_ref, sin_ref,
                  kvn_ref, kr_ref, lf_ref, *, kvl):
    acc = _dot_nt(h_ref[...], w_ref[...])
    kvn_ref[...] = _rms(acc[:, :kvl], g_ref[...]).astype(kvn_ref.dtype)
    r = acc[:, kvl:kvl + LANES]
    p = acc[:, kvl + LANES:kvl + 2 * LANES]
    kr_ref[...] = (r * cos_ref[...] + p * sin_ref[...]).astype(kr_ref.dtype)
    lf_ref[...] = _log_sigmoid(acc[:, kvl + 2 * LANES:kvl + 3 * LANES] + bf_ref[...])


def inproj_small(h, w, kv_norm, b_forget_pad, cos_t, sin_t, seq, name):
    m, d = h.shape
    kvl = kv_norm.shape[0]
    n = w.shape[0]
    tm = _pick(seq, 512)
    nseq = seq // tm
    return pl.pallas_call(
        functools.partial(_small_kernel, kvl=kvl),
        out_shape=(jax.ShapeDtypeStruct((m, kvl), BF16),
                   jax.ShapeDtypeStruct((m, LANES), BF16),
                   jax.ShapeDtypeStruct((m, LANES), F32)),
        grid=(m // tm,),
        in_specs=[pl.BlockSpec((tm, d), lambda i: (i, 0)),
                  pl.BlockSpec((n, d), lambda i: (0, 0)),
                  pl.BlockSpec((1, kvl), lambda i: (0, 0)),
                  pl.BlockSpec((1, LANES), lambda i: (0, 0)),
                  pl.BlockSpec((tm, LANES), lambda i: (i % nseq, 0)),
                  pl.BlockSpec((tm, LANES), lambda i: (i % nseq, 0))],
        out_specs=(pl.BlockSpec((tm, kvl), lambda i: (i, 0)),
                   pl.BlockSpec((tm, LANES), lambda i: (i, 0)),
                   pl.BlockSpec((tm, LANES), lambda i: (i, 0))),
        compiler_params=_cparams(("parallel",)),
        name=name,
    )(h, w, kv_norm.reshape(1, kvl), b_forget_pad, cos_t, sin_t)


def _mlaq_kernel(x_ref, wt_ref, cos_ref, sin_ref, o_ref, *, heads, qscale):
    hw = heads * HEAD_DIM
    x = x_ref[...]
    nope = _dot_nt(wt_ref[:hw, :], x)
    o_ref[0, :hw, :] = (nope * qscale).astype(o_ref.dtype)
    r = _dot_nt(wt_ref[hw:2 * hw, :], x)
    p = _dot_nt(wt_ref[2 * hw:, :], x)
    cos_t, sin_t = cos_ref[...] * qscale, sin_ref[...] * qscale
    for h in range(heads):
        sl = slice(h * HEAD_DIM, (h + 1) * HEAD_DIM)
        o_ref[0, hw + h * HEAD_DIM:hw + (h + 1) * HEAD_DIM, :] = (
            r[sl, :] * cos_t + p[sl, :] * sin_t).astype(o_ref.dtype)


def mla_q(qn, wt, cos_tt, sin_tt, seq, heads, qscale, t, name):
    m, ql = qn.shape
    n = wt.shape[0]
    nseq = seq // t
    return pl.pallas_call(
        functools.partial(_mlaq_kernel, heads=heads, qscale=qscale),
        out_shape=jax.ShapeDtypeStruct((m // t, 2 * heads * HEAD_DIM, t), BF16),
        grid=(m // t,),
        in_specs=[pl.BlockSpec((t, ql), lambda i: (i, 0)),
                  pl.BlockSpec((n, ql), lambda i: (0, 0)),
                  pl.BlockSpec((LANES, t), lambda i: (0, i % nseq)),
                  pl.BlockSpec((LANES, t), lambda i: (0, i % nseq))],
        out_specs=pl.BlockSpec((1, 2 * heads * HEAD_DIM, t), lambda i: (i, 0, 0)),
        compiler_params=_cparams(("parallel",)),
        name=name,
    )(qn, wt, cos_tt, sin_tt)


def _split(x, parts):
    out = []
    for _ in range(parts - 1):
        hi = x.astype(BF16)
        out.append(hi)
        x = x - hi.astype(F32)
    out.append(x.astype(BF16))
    return out


def _cumsum_kernel(x_ref, o_ref, *, seq, tc):
    rows = lax.broadcasted_iota(jnp.int32, (tc, tc), 0)
    cols = lax.broadcasted_iota(jnp.int32, (tc, tc), 1)
    tri = jnp.where(cols <= rows, 1.0, 0.0).astype(BF16)

    def body(c, carry):
        r0 = pl.multiple_of(c * tc, tc)
        x = x_ref[0, pl.ds(r0, tc), :]
        acc = jnp.zeros((tc, LANES), F32)
        for part in _split(x, 3):
            acc = acc + jnp.dot(tri, part, preferred_element_type=F32)
        acc = acc + carry
        o_ref[0, pl.ds(r0, tc), :] = acc * LOG2E
        return acc[tc - 1:tc, :]

    lax.fori_loop(0, seq // tc, body, jnp.zeros((1, LANES), F32))


def cumsum_rows(x, name):
    b, seq, n = x.shape
    tc = _pick(seq, 256)
    return pl.pallas_call(
        functools.partial(_cumsum_kernel, seq=seq, tc=tc),
        out_shape=jax.ShapeDtypeStruct((b, seq, n), F32),
        grid=(b,),
        in_specs=[pl.BlockSpec((1, seq, n), lambda i: (i, 0, 0))],
        out_specs=pl.BlockSpec((1, seq, n), lambda i: (i, 0, 0)),
        compiler_params=_cparams(("parallel",)),
        name=name,
    )(x)


def _head(ref, i, rows=slice(None)):
    return ref[rows, i * HEAD_DIM:(i + 1) * HEAD_DIM]


def _head_t(ref, i):
    return ref[0, i * HEAD_DIM:(i + 1) * HEAD_DIM, :]


def _pipeline(n, chunk_of, scores_into, consume, buf_a, buf_b, mask_last):
    scores_into(buf_a, chunk_of(0))

    def pair(p, _):
        scores_into(buf_b, chunk_of(2 * p + 1))
        consume(buf_a, chunk_of(2 * p), False)
        scores_into(buf_a, chunk_of(2 * p + 2))
        consume(buf_b, chunk_of(2 * p + 1), False)
        return 0

    lax.fori_loop(0, (n - 1) // 2, pair, 0)

    @pl.when(n % 2 == 1)
    def _():
        consume(buf_a, chunk_of(n - 1), mask_last)

    @pl.when(n % 2 == 0)
    def _():
        scores_into(buf_b, chunk_of(n - 1))
        consume(buf_a, chunk_of(n - 2), False)
        consume(buf_b, chunk_of(n - 1), mask_last)


def _softmax_consume(buf, vts, mask, m_ref, l_ref, acc_ref):
    g = len(vts)
    ps, alphas = [], []
    for i in range(g):
        s = buf[i] if mask is None else jnp.where(mask, buf[i], NEG_BIG)
        m_old = m_ref[i]
        m_new = jnp.maximum(m_old, jnp.max(s, axis=0, keepdims=True))
        alpha = jnp.exp2(m_old - m_new)
        p = jnp.exp2(s - m_new)
        m_ref[i] = m_new
        l_ref[i] = alpha * l_ref[i] + jnp.sum(p, axis=0, keepdims=True)
        alphas.append(alpha)
        ps.append(p.astype(BF16))
    pvs = [jnp.dot(vt, p, preferred_element_type=F32) for vt, p in zip(vts, ps)]
    for i in range(g):
        acc_ref[i] = alphas[i] * acc_ref[i] + pvs[i]


def _softmax_init(m_ref, l_ref, acc_ref):
    m_ref[...] = jnp.full(m_ref.shape, NEG_BIG, F32)
    l_ref[...] = jnp.zeros(l_ref.shape, F32)
    acc_ref[...] = jnp.zeros(acc_ref.shape, F32)


def _softmax_store(o_ref, l_ref, acc_ref):
    for i in range(acc_ref.shape[0]):
        o_ref[:, i * HEAD_DIM:(i + 1) * HEAD_DIM] = (
            acc_ref[i] * (1.0 / l_ref[i])).T.astype(o_ref.dtype)


def _attn_scratch(g, t):
    return [pltpu.VMEM((g, t, t), F32), pltpu.VMEM((g, t, t), F32),
            pltpu.VMEM((g, 1, t), F32), pltpu.VMEM((g, 1, t), F32),
            pltpu.VMEM((g, HEAD_DIM, t), F32)]


def _mla_kernel(qn_ref, qr_ref, kn_ref, kr_ref, vt_ref, o_ref,
                sa_ref, sb_ref, m_ref, l_ref, acc_ref, *, t, g):
    qi = pl.program_id(2)
    qts = [jnp.concatenate([_head_t(qn_ref, i), _head_t(qr_ref, i)], axis=0) for i in range(g)]
    keys = lax.broadcasted_iota(jnp.int32, (t, t), 0) // CHUNK
    queries = lax.broadcasted_iota(jnp.int32, (t, t), 1) // CHUNK
    mask = keys <= queries

    def scores_into(buf, j):
        rows = pl.ds(pl.multiple_of(j * t, t), t)
        kr = kr_ref[rows, :]
        for i in range(g):
            buf[i] = jnp.dot(jnp.concatenate([_head(kn_ref, i, rows), kr], axis=-1), qts[i],
                             preferred_element_type=F32)

    def consume(buf, j, masked):
        vts = [vt_ref[j, i * HEAD_DIM:(i + 1) * HEAD_DIM, :] for i in range(g)]
        _softmax_consume(buf, vts, mask if masked else None, m_ref, l_ref, acc_ref)

    _softmax_init(m_ref, l_ref, acc_ref)
    _pipeline(qi + 1, lambda n: n, scores_into, consume, sa_ref, sb_ref, True)
    _softmax_store(o_ref, l_ref, acc_ref)


def mla_attention(q, kn, kr, vt, batch, seq, heads, name):
    m = kn.shape[0]
    t = vt.shape[2]
    nq = seq // t
    g = _head_group(heads)
    gw, hb = g * HEAD_DIM, heads // g
    return pl.pallas_call(
        functools.partial(_mla_kernel, t=t, g=g),
        out_shape=jax.ShapeDtypeStruct((m, heads * HEAD_DIM), BF16),
        grid=(batch, hb, nq),
        in_specs=[pl.BlockSpec((1, gw, t), lambda b, h, i: (b * nq + i, h, 0)),
                  pl.BlockSpec((1, gw, t), lambda b, h, i: (b * nq + i, hb + h, 0)),
                  pl.BlockSpec((seq, gw), lambda b, h, i: (b, h)),
                  pl.BlockSpec((seq, LANES), lambda b, h, i: (b, 0)),
                  pl.BlockSpec((nq, gw, t), lambda b, h, i: (b, h, 0))],
        out_specs=pl.BlockSpec((t, gw), lambda b, h, i: (b * nq + i, h)),
        scratch_shapes=_attn_scratch(g, t),
        compiler_params=_cparams(("parallel", "parallel", "arbitrary")),
        name=name,
    )(q, q, kn, kr, vt)


def _fox_kernel(qt_ref, k_ref, vt_ref, cq_ref, ck_ref, o_ref,
                sa_ref, sb_ref, m_ref, l_ref, acc_ref, *, t, g):
    qi = pl.program_id(2)
    qts = [_head_t(qt_ref, i) for i in range(g)]
    cqs = [cq_ref[i] for i in range(g)]
    keys = lax.broadcasted_iota(jnp.int32, (t, t), 0)
    queries = lax.broadcasted_iota(jnp.int32, (t, t), 1)
    mask = keys <= queries

    def scores_into(buf, j):
        rows = pl.ds(pl.multiple_of(j * t, t), t)
        for i in range(g):
            ck = ck_ref[i, rows, :]
            ck = jnp.concatenate([ck] * (t // LANES), axis=1)
            buf[i] = jnp.dot(_head(k_ref, i, rows), qts[i],
                             preferred_element_type=F32) + cqs[i] - ck

    def consume(buf, j, masked):
        vts = [vt_ref[j, i * HEAD_DIM:(i + 1) * HEAD_DIM, :] for i in range(g)]
        _softmax_consume(buf, vts, mask if masked else None, m_ref, l_ref, acc_ref)

    _softmax_init(m_ref, l_ref, acc_ref)
    _pipeline(qi + 1, lambda n: n, scores_into, consume, sa_ref, sb_ref, True)
    _softmax_store(o_ref, l_ref, acc_ref)


def fox_attention(qt, k, vt, c_row, c_rep, batch, seq, heads, name):
    m = k.shape[0]
    t = vt.shape[2]
    nq = seq // t
    g = _head_group(heads)
    gw, hb = g * HEAD_DIM, heads // g
    return pl.pallas_call(
        functools.partial(_fox_kernel, t=t, g=g),
        out_shape=jax.ShapeDtypeStruct((m, heads * HEAD_DIM), BF16),
        grid=(batch, hb, nq),
        in_specs=[pl.BlockSpec((1, gw, t), lambda b, h, i: (b * nq + i, h, 0)),
                  pl.BlockSpec((seq, gw), lambda b, h, i: (b, h)),
                  pl.BlockSpec((nq, gw, t), lambda b, h, i: (b, h, 0)),
                  pl.BlockSpec((g, 1, t), lambda b, h, i: (b * hb + h, 0, i)),
                  pl.BlockSpec((g, seq, LANES), lambda b, h, i: (b * hb + h, 0, 0))],
        out_specs=pl.BlockSpec((t, gw), lambda b, h, i: (b * nq + i, h)),
        scratch_shapes=_attn_scratch(g, t),
        compiler_params=_cparams(("parallel", "parallel", "arbitrary")),
        name=name,
    )(qt, k, vt, c_row, c_rep)


def _sb_kernel(qt_ref, k_ref, vt_ref, o_ref, za_ref, zb_ref, rest_ref, acc_ref, *, t, g):
    qi = pl.program_id(2)
    qts = [_head_t(qt_ref, i) for i in range(g)]
    keys = lax.broadcasted_iota(jnp.int32, (t, t), 0)
    queries = lax.broadcasted_iota(jnp.int32, (t, t), 1)
    upper = jnp.where(queries >= keys, 1.0, 0.0).astype(BF16)
    strict = keys < queries

    def scores_into(buf, j):
        rows = pl.ds(pl.multiple_of(j * t, t), t)
        for i in range(g):
            buf[i] = jnp.dot(_head(k_ref, i, rows), qts[i],
                             preferred_element_type=F32)

    def consume(buf, j, masked):
        loms = []
        for i in range(g):
            z = buf[i]
            nz = -z
            lom = jnp.minimum(nz, 0.0) - jnp.log(1.0 + jnp.exp2(jnp.minimum(z, nz))) * LOG2E
            loms.append(jnp.where(strict, lom, 0.0) if masked else lom)
        incs = [sum(jnp.dot(upper, part, preferred_element_type=F32) for part in _split(lom, 2))
                for lom in loms]
        avs = []
        for i in range(g):
            a = jnp.exp2(buf[i] + incs[i] + rest_ref[i])
            avs.append((jnp.where(strict, a, 0.0) if masked else a).astype(BF16))
            rest_ref[i] = rest_ref[i] + incs[i][0:1, :]
        pvs = [jnp.dot(vt_ref[j, i * HEAD_DIM:(i + 1) * HEAD_DIM, :], avs[i],
                       preferred_element_type=F32) for i in range(g)]
        for i in range(g):
            acc_ref[i] = acc_ref[i] + pvs[i]

    rest_ref[...] = jnp.zeros(rest_ref.shape, F32)
    acc_ref[...] = jnp.zeros(acc_ref.shape, F32)
    scores_into(za_ref, qi)
    consume(za_ref, qi, True)

    @pl.when(qi > 0)
    def _():
        _pipeline(qi, lambda n: qi - 1 - n, scores_into, consume, za_ref, zb_ref, False)

    for i in range(g):
        o_ref[:, i * HEAD_DIM:(i + 1) * HEAD_DIM] = acc_ref[i].T.astype(o_ref.dtype)


def sb_attention(qt, k, vt, batch, seq, heads, name):
    m = k.shape[0]
    t = vt.shape[2]
    nq = seq // t
    g = _head_group(heads)
    gw, hb = g * HEAD_DIM, heads // g
    return pl.pallas_call(
        functools.partial(_sb_kernel, t=t, g=g),
        out_shape=jax.ShapeDtypeStruct((m, heads * HEAD_DIM), BF16),
        grid=(batch, hb, nq),
        in_specs=[pl.BlockSpec((1, gw, t), lambda b, h, i: (b * nq + i, h, 0)),
                  pl.BlockSpec((seq, gw), lambda b, h, i: (b, h)),
                  pl.BlockSpec((nq, gw, t), lambda b, h, i: (b, h, 0))],
        out_specs=pl.BlockSpec((t, gw), lambda b, h, i: (b * nq + i, h)),
        scratch_shapes=[pltpu.VMEM((g, t, t), F32), pltpu.VMEM((g, t, t), F32),
                        pltpu.VMEM((g, 1, t), F32), pltpu.VMEM((g, HEAD_DIM, t), F32)],
        compiler_params=_cparams(("parallel", "parallel", "arbitrary")),
        name=name,
    )(qt, k, vt)


def _merge_kernel(oa_ref, ob_ref, oc_ref, wa_ref, wb_ref, wc_ref,
                  ga_ref, gb_ref, gc_ref, o_ref, wa_s, wb_s, wc_s):
    @pl.when(pl.program_id(1) == 0)
    def _():
        wa_s[...] = _wblock(wa_ref).astype(BF16)
        wb_s[...] = _wblock(wb_ref).astype(BF16)
        wc_s[...] = _wblock(wc_ref).astype(BF16)

    acc = ga_ref[...].astype(F32) * jnp.dot(oa_ref[...], wa_s[...], preferred_element_type=F32)
    acc += gb_ref[...].astype(F32) * jnp.dot(ob_ref[...], wb_s[...], preferred_element_type=F32)
    acc += gc_ref[...].astype(F32) * jnp.dot(oc_ref[...], wc_s[...], preferred_element_type=F32)
    o_ref[...] = acc.astype(o_ref.dtype)


def gated_merge(oa, ob, oc, wa, wb, wc, gates, layer, name):
    m = oa.shape[0]
    d = wa.shape[-1]
    tm, tn = _pick(m, 1024), _pick(d, 512)
    nj = d // tn
    row = lambda a: pl.BlockSpec((tm, a.shape[1]), lambda j, i: (i, 0))
    col = lambda a: _wspec(a, layer, (a.shape[-2], tn), lambda j, i: (0, j))
    gate = lambda g: pl.BlockSpec((tm, tn), lambda j, i: (i, g * nj + j))
    return pl.pallas_call(
        _merge_kernel,
        out_shape=jax.ShapeDtypeStruct((m, d), BF16),
        grid=(nj, m // tm),
        in_specs=[row(oa), row(ob), row(oc), col(wa), col(wb), col(wc),
                  gate(0), gate(1), gate(2)],
        out_specs=pl.BlockSpec((tm, tn), lambda j, i: (i, j)),
        scratch_shapes=[pltpu.VMEM((a.shape[-2], tn), BF16) for a in (wa, wb, wc)],
        compiler_params=_cparams(("parallel", "arbitrary")),
        name=name,
    )(oa, ob, oc, wa, wb, wc, gates, gates, gates)


def _ffn_up_kernel(h_ref, wg_ref, wu_ref, cw_ref, cb_ref, o_ref, carry_ref, wg_s, wu_s,
                   *, tiles_per_seq):
    i = pl.program_id(1)

    @pl.when(i == 0)
    def _():
        wg_s[...] = _wblock(wg_ref).astype(BF16)
        wu_s[...] = _wblock(wu_ref).astype(BF16)

    @pl.when(i % tiles_per_seq == 0)
    def _():
        carry_ref[...] = jnp.zeros_like(carry_ref)

    prev = carry_ref[...]
    g = jnp.dot(h_ref[...], wg_s[...], preferred_element_type=F32)
    u = jnp.dot(h_ref[...], wu_s[...], preferred_element_type=F32)
    tm = g.shape[0]
    carry_ref[...] = g[tm - 8:, :]
    row = lax.broadcasted_iota(jnp.int32, g.shape, 0)
    g1 = jnp.where(row == 0, prev[7:8, :], pltpu.roll(g, 1, 0))
    g2 = jnp.where(row == 0, prev[6:7, :],
                   jnp.where(row == 1, prev[7:8, :], pltpu.roll(g, 2, 0)))
    cw = cw_ref[...]
    conv = cb_ref[...] + cw[0:1, :] * g2 + cw[1:2, :] * g1 + cw[2:3, :] * g
    act = conv * (1.0 / (1.0 + jnp.exp(-conv)))
    o_ref[...] = (act * u).astype(o_ref.dtype)


def ffn_up(h, wg, wu, conv_w, conv_b, seq, layer, name):
    m, d = h.shape
    f = wg.shape[-1]
    tm, tn = _pick(seq, 512), _pick(f, 512)
    return pl.pallas_call(
        functools.partial(_ffn_up_kernel, tiles_per_seq=seq // tm),
        out_shape=jax.ShapeDtypeStruct((m, f), BF16),
        grid=(f // tn, m // tm),
        in_specs=[pl.BlockSpec((tm, d), lambda j, i: (i, 0)),
                  _wspec(wg, layer, (d, tn), lambda j, i: (0, j)),
                  _wspec(wu, layer, (d, tn), lambda j, i: (0, j)),
                  pl.BlockSpec((CONV_WIDTH, tn), lambda j, i: (0, j)),
                  pl.BlockSpec((1, tn), lambda j, i: (0, j))],
        out_specs=pl.BlockSpec((tm, tn), lambda j, i: (i, j)),
        scratch_shapes=[pltpu.VMEM((8, tn), F32), pltpu.VMEM((d, tn), BF16),
                        pltpu.VMEM((d, tn), BF16)],
        compiler_params=_cparams(("arbitrary", "arbitrary")),
        name=name,
    )(h, wg, wu, conv_w, conv_b.reshape(1, f))


def _rope_tables(seq):
    inv = 1.0 / (ROPE_THETA ** (jnp.arange(0, ROPE, 2, dtype=F32) / ROPE))
    ang = jnp.arange(seq, dtype=F32)[:, None] * inv[None, :]
    zeros = jnp.zeros((seq, LANES - ROPE), F32)
    cos_t = jnp.concatenate([jnp.cos(ang), jnp.cos(ang), zeros], axis=-1)
    sin_t = jnp.concatenate([jnp.sin(ang), jnp.sin(ang), zeros], axis=-1)
    return cos_t, sin_t


def _rope_partner(w):
    half = ROPE // 2
    return jnp.concatenate([-w[..., half:], w[..., :half]], axis=-1)


def _in_offsets(d):
    ql, kvl = d // 4, d // 8
    hf = d // (4 * HEAD_DIM)
    fw = hf * HEAD_DIM
    o_kv = ql
    o_kr = o_kv + kvl
    o_f = o_kr + ROPE
    o_fpre = o_f + 3 * fw
    o_s = o_fpre + hf
    o_g = o_s + 3 * fw
    return dict(kv=o_kv, kr=o_kr, f=o_f, fpre=o_fpre, s=o_s, g=o_g)


def _small_weight(win_kv, win_kr, win_fp, hf):
    d = win_kv.shape[1]
    zpad = jnp.zeros((LANES - ROPE, d), BF16)
    kr = win_kr[:ROPE]
    half = ROPE // 2
    partner = jnp.concatenate([-kr[half:], kr[:half]], axis=0)
    return jnp.concatenate([win_kv, kr, zpad, partner, zpad,
                            win_fp[:hf], jnp.zeros((LANES - hf, d), BF16)], axis=0)


def _layer_weights(d, w_uq, w_ukv):
    ql, kvl = d // 4, d // 8
    hm = d // (2 * HEAD_DIM)
    uq = w_uq.reshape(ql, hm, HEAD_DIM + ROPE)
    r = uq[:, :, HEAD_DIM:]
    z = jnp.zeros((ql, hm, LANES - ROPE), F32)
    w_q = jnp.concatenate(
        [uq[:, :, :HEAD_DIM].reshape(ql, hm * HEAD_DIM),
         jnp.concatenate([r, z], axis=-1).reshape(ql, hm * HEAD_DIM),
         jnp.concatenate([_rope_partner(r), z], axis=-1).reshape(ql, hm * HEAD_DIM)],
        axis=-1).T.astype(BF16)
    ukv = w_ukv.reshape(kvl, hm, 2 * HEAD_DIM)
    w_kv = jnp.concatenate([ukv[:, :, :HEAD_DIM].reshape(kvl, hm * HEAD_DIM),
                            ukv[:, :, HEAD_DIM:].reshape(kvl, hm * HEAD_DIM)], axis=-1).astype(BF16)
    return dict(uq=w_q, ukv=w_kv)


def kernel(x, attn_norm, w_in, b_forget, b_gate, q_norm, w_uq, kv_norm, w_ukv, w_br_mla, w_br_fox, w_br_sb, w_o, ffn_norm, w_ffn_gate, conv_w, conv_b, w_ffn_up, w_ffn_down, final_norm):
    batch, seq, d = x.shape
    depth = w_in.shape[0]
    m = batch * seq
    hm, hf = d // (2 * HEAD_DIM), d // (4 * HEAD_DIM)
    hs = hf
    fw = hf * HEAD_DIM
    t_att = _pick(seq, ATT_TILE)
    cos_t, sin_t = _rope_tables(seq)
    cos_tt, sin_tt = cos_t.T, sin_t.T
    xf = x.reshape(m, d)
    q_scale = LOG2E / math.sqrt(HEAD_DIM)
    mla_scale = LOG2E / math.sqrt(HEAD_DIM + ROPE)
    off = _in_offsets(d)

    w_in_t = jnp.swapaxes(w_in, 1, 2)
    w_down = w_ffn_down.astype(BF16)

    for l in range(depth):
        w = _layer_weights(d, w_uq[l], w_ukv[l])
        w_small = _small_weight(window_cast(w_in_t, l, off["kv"], d // 8, f"w_kvlat_{l}"),
                                window_cast(w_in_t, l, off["kr"], LANES, f"w_krope_{l}"),
                                window_cast(w_in_t, l, off["fpre"], LANES, f"w_fpre_{l}"), hf)
        w_ql = window_cast(w_in_t, l, 0, d // 4, f"w_qlat_{l}")
        w_f = window_cast(w_in_t, l, off["f"], 3 * fw, f"w_fox_{l}")
        w_s = window_cast(w_in_t, l, off["s"], 3 * fw, f"w_sb_{l}")
        w_g = window_cast(w_in_t, l, off["g"], 3 * d, f"w_gate_{l}")
        h = rmsnorm(xf, attn_norm[l], BF16, f"attn_norm_{l}")

        qn = matmul(h, w_ql, out_dtype=BF16, name=f"inproj_qlat_{l}", wt=True,
                    epilogue=_ep_rmsnorm, extras=[(q_norm[l].reshape(1, -1), "row")],
                    tn=w_ql.shape[0])
        bf_pad = jnp.zeros((1, LANES), F32).at[0, :hf].set(b_forget[l])
        kvn, kr, logf = inproj_small(h, w_small, kv_norm[l], bf_pad, cos_t, sin_t, seq,
                                     f"inproj_small_{l}")
        qt_f = matmul_tout(h, w_f, t_att, f"inproj_fqt_{l}", n_out=fw, wt=True, scale=q_scale)
        k_f = matmul(h, w_f, out_dtype=BF16, name=f"inproj_fk_{l}", col0=fw, n_out=fw, wt=True,
                     tn=1024)
        vt_f = matmul_tout(h, w_f, t_att, f"inproj_fvt_{l}", col0=2 * fw, wt=True)
        qt_s = matmul_tout(h, w_s, t_att, f"inproj_sqt_{l}", n_out=fw, wt=True, scale=q_scale)
        k_s = matmul(h, w_s, out_dtype=BF16, name=f"inproj_sk_{l}", col0=fw, n_out=fw, wt=True,
                     tn=1024)
        vt_s = matmul_tout(h, w_s, t_att, f"inproj_svt_{l}", col0=2 * fw, wt=True)
        gates = matmul(h, w_g, out_dtype=BF16, name=f"inproj_gate_{l}", wt=True,
                       epilogue=_ep_sigmoid_bias, extras=[(b_gate[l].reshape(1, -1), "row")],
                       tn=1024)

        q = mla_q(qn, w["uq"], cos_tt, sin_tt, seq, hm, mla_scale, t_att, f"mla_q_{l}")
        kn = matmul(kvn, w["ukv"], out_dtype=BF16, name=f"mla_k_{l}", n_out=hm * HEAD_DIM)
        vt_mla = matmul_tout(kvn, w["ukv"], t_att, f"mla_vt_{l}", col0=hm * HEAD_DIM)
        o_mla = mla_attention(q, kn, kr, vt_mla, batch, seq, hm, f"mla_attn_{l}")

        c = cumsum_rows(logf.reshape(batch, seq, LANES), f"fox_cumsum_{l}")
        c_heads = jnp.transpose(c[:, :, :hf], (0, 2, 1)).reshape(batch * hf, seq)
        c_rep = jnp.broadcast_to(c_heads[:, :, None], (batch * hf, seq, LANES))
        o_fox = fox_attention(qt_f, k_f, vt_f, c_heads.reshape(batch * hf, 1, seq), c_rep,
                              batch, seq, hf, f"fox_attn_{l}")

        o_sb = sb_attention(qt_s, k_s, vt_s, batch, seq, hs, f"sb_attn_{l}")

        merged = gated_merge(o_mla, o_fox, o_sb, w_br_mla, w_br_fox, w_br_sb, gates, l,
                             f"merge_{l}")
        xf = matmul_ws(merged, w_o, out_dtype=F32, name=f"out_proj_{l}", layer=l,
                       epilogue=_ep_residual, extras=[(xf, "tile")])

        h = rmsnorm(xf, ffn_norm[l], BF16, f"ffn_norm_{l}")
        act = ffn_up(h, w_ffn_gate, w_ffn_up, conv_w[l], conv_b[l], seq, l, f"ffn_up_{l}")
        xf = matmul(act, w_down, out_dtype=F32, name=f"ffn_down_{l}", layer=l,
                    epilogue=_ep_residual, extras=[(xf, "tile")], tk=4096)

    return rmsnorm(xf, final_norm, F32, "final_norm").reshape(batch, seq, d)
```

```python
import functools
import math

import jax
import jax.numpy as jnp
from jax import lax
from jax.experimental import pallas as pl
from jax.experimental.pallas import tpu as pltpu

HEAD_DIM = 128
CHUNK = 64
ROPE = 64
ROPE_THETA = 10000.0
CONV_WIDTH = 3
EPS = 1e-6
LANES = 128
NEG_BIG = -1e30
LOG2E = math.log2(math.e)
VMEM_LIMIT = 56 * 1024 * 1024
ATT_TILE = 256
ATT_HEADS = 4
SUM_ROWS = 16

F32 = jnp.float32
BF16 = jnp.bfloat16


def _cparams(sem):
    return pltpu.CompilerParams(dimension_semantics=sem, vmem_limit_bytes=VMEM_LIMIT)


def _pick(n, pref):
    t = min(n, pref)
    while n % t:
        t //= 2
    return t


def _head_group(heads):
    g = ATT_HEADS
    while heads % g:
        g //= 2
    return g


def _rms(x, g):
    return x * lax.rsqrt(jnp.mean(x * x, axis=-1, keepdims=True) + EPS) * g


def _sigmoid(x):
    return 1.0 / (1.0 + jnp.exp(-x))


def _log_sigmoid(x):
    return jnp.minimum(x, 0.0) - jnp.log(1.0 + jnp.exp(-jnp.abs(x)))


def _rmsnorm_kernel(x_ref, g_ref, o_ref):
    o_ref[...] = _rms(x_ref[...], g_ref[...]).astype(o_ref.dtype)


def rmsnorm(x, g, out_dtype, name):
    m, d = x.shape
    tm = _pick(m, 256)
    return pl.pallas_call(
        _rmsnorm_kernel,
        out_shape=jax.ShapeDtypeStruct((m, d), out_dtype),
        grid=(m // tm,),
        in_specs=[pl.BlockSpec((tm, d), lambda i: (i, 0)),
                  pl.BlockSpec((1, d), lambda i: (0, 0))],
        out_specs=pl.BlockSpec((tm, d), lambda i: (i, 0)),
        compiler_params=_cparams(("parallel",)),
        name=name,
    )(x, g.reshape(1, d))


def _dot_nt(a, b):
    return lax.dot_general(a, b, (((1,), (1,)), ((), ())), preferred_element_type=F32)


def _wblock(ref):
    return ref[0] if len(ref.shape) == 3 else ref[...]


def _wspec(w, layer, shape, index):
    if layer is None:
        return pl.BlockSpec(shape, index)
    return pl.BlockSpec((1,) + shape, lambda *g: (layer,) + index(*g))


def _mm_kernel(*refs, nk, n_extra, epilogue, wt):
    x_ref, w_ref = refs[0], refs[1]
    extras = refs[2:2 + n_extra]
    o_ref = refs[2 + n_extra]
    if nk == 1:
        if wt:
            acc = _dot_nt(x_ref[...], _wblock(w_ref))
        else:
            acc = jnp.dot(x_ref[...], _wblock(w_ref), preferred_element_type=F32)
        o_ref[...] = epilogue(acc, *extras).astype(o_ref.dtype)
        return
    acc_ref = refs[3 + n_extra]
    k = pl.program_id(2)

    @pl.when(k == 0)
    def _():
        acc_ref[...] = jnp.dot(x_ref[...], _wblock(w_ref), preferred_element_type=F32)

    @pl.when((k > 0) & (k < nk - 1))
    def _():
        acc_ref[...] += jnp.dot(x_ref[...], _wblock(w_ref), preferred_element_type=F32)

    @pl.when(k == nk - 1)
    def _():
        acc = acc_ref[...] + jnp.dot(x_ref[...], _wblock(w_ref), preferred_element_type=F32)
        o_ref[...] = epilogue(acc, *extras).astype(o_ref.dtype)


def matmul(x, w, *, out_dtype, name, epilogue=None, extras=(), tm=1024, tn=512, tk=None,
           col0=0, n_out=None, wt=False, layer=None):
    m, kd = x.shape
    n = w.shape[-2 if wt else -1] - col0 if n_out is None else n_out
    assert not (wt and tk is not None)
    tm, tn = _pick(m, tm), _pick(math.gcd(n, col0), tn)
    cb = col0 // tn
    tk = kd if tk is None else _pick(kd, tk)
    nk = kd // tk
    if epilogue is None:
        epilogue = lambda acc: acc
    in_specs = [pl.BlockSpec((tm, tk), lambda i, j, k: (i, k)),
                _wspec(w, layer, (tn, tk), lambda i, j, k: (cb + j, k)) if wt else
                _wspec(w, layer, (tk, tn), lambda i, j, k: (k, cb + j))]
    args = [x, w]
    for arr, kind in extras:
        if kind == "row":
            in_specs.append(pl.BlockSpec((1, tn), lambda i, j, k: (0, j)))
        else:
            in_specs.append(pl.BlockSpec((tm, tn), lambda i, j, k: (i, j)))
        args.append(arr)
    scratch = [pltpu.VMEM((tm, tn), F32)] if nk > 1 else []
    return pl.pallas_call(
        functools.partial(_mm_kernel, nk=nk, n_extra=len(extras), epilogue=epilogue, wt=wt),
        out_shape=jax.ShapeDtypeStruct((m, n), out_dtype),
        grid=(m // tm, n // tn, nk),
        in_specs=in_specs,
        out_specs=pl.BlockSpec((tm, tn), lambda i, j, k: (i, j)),
        scratch_shapes=scratch,
        compiler_params=_cparams(("parallel", "parallel", "arbitrary")),
        name=name,
    )(*args)


def _mm_ws_kernel(*refs, n_extra, epilogue):
    x_ref, w_ref = refs[0], refs[1]
    extras = refs[2:2 + n_extra]
    o_ref, w_s = refs[2 + n_extra], refs[3 + n_extra]

    @pl.when(pl.program_id(1) == 0)
    def _():
        w_s[...] = _wblock(w_ref).astype(BF16)

    acc = jnp.dot(x_ref[...], w_s[...], preferred_element_type=F32)
    o_ref[...] = epilogue(acc, *extras).astype(o_ref.dtype)


def matmul_ws(x, w, *, out_dtype, name, epilogue=None, extras=(), tm=1024, tn=512, layer=None):
    m, kd = x.shape
    n = w.shape[-1]
    tm, tn = _pick(m, tm), _pick(n, tn)
    if epilogue is None:
        epilogue = lambda acc: acc
    in_specs = [pl.BlockSpec((tm, kd), lambda j, i: (i, 0)),
                _wspec(w, layer, (kd, tn), lambda j, i: (0, j))]
    args = [x, w]
    for arr, kind in extras:
        if kind == "row":
            in_specs.append(pl.BlockSpec((1, tn), lambda j, i: (0, j)))
        else:
            in_specs.append(pl.BlockSpec((tm, tn), lambda j, i: (i, j)))
        args.append(arr)
    return pl.pallas_call(
        functools.partial(_mm_ws_kernel, n_extra=len(extras), epilogue=epilogue),
        out_shape=jax.ShapeDtypeStruct((m, n), out_dtype),
        grid=(n // tn, m // tm),
        in_specs=in_specs,
        out_specs=pl.BlockSpec((tm, tn), lambda j, i: (i, j)),
        scratch_shapes=[pltpu.VMEM((kd, tn), BF16)],
        compiler_params=_cparams(("parallel", "arbitrary")),
        name=name,
    )(*args)


def _ep_rmsnorm(acc, g_ref):
    return _rms(acc, g_ref[...])


def _ep_sigmoid_bias(acc, b_ref):
    return _sigmoid(acc + b_ref[...])


def _ep_residual(acc, r_ref):
    return r_ref[...] + acc


def _mm_tout_kernel(x_ref, w_ref, o_ref, *, t, wt, scale):
    if wt:
        res = _dot_nt(w_ref[...], x_ref[...]) * scale
        for c in range(o_ref.shape[0]):
            o_ref[c] = res[:, c * t:(c + 1) * t].astype(o_ref.dtype)
        return
    res = jnp.dot(x_ref[...], w_ref[...], preferred_element_type=F32) * scale
    for c in range(o_ref.shape[0]):
        o_ref[c] = res[c * t:(c + 1) * t, :].T.astype(o_ref.dtype)


def matmul_tout(x, w, t, name, *, col0=0, n_out=None, wt=False, scale=1.0):
    m, kd = x.shape
    n = w.shape[0 if wt else 1] - col0 if n_out is None else n_out
    tm, tn = _pick(m, 1024), _pick(math.gcd(n, col0), 512)
    cb = col0 // tn
    return pl.pallas_call(
        functools.partial(_mm_tout_kernel, t=t, wt=wt, scale=scale),
        out_shape=jax.ShapeDtypeStruct((m // t, n, t), BF16),
        grid=(m // tm, n // tn),
        in_specs=[pl.BlockSpec((tm, kd), lambda i, j: (i, 0)),
                  pl.BlockSpec((tn, kd), lambda i, j: (cb + j, 0)) if wt else
                  pl.BlockSpec((kd, tn), lambda i, j: (0, cb + j))],
        out_specs=pl.BlockSpec((tm // t, tn, t), lambda i, j: (i, j, 0)),
        compiler_params=_cparams(("parallel", "parallel")),
        name=name,
    )(x, w)


def _window_cast_kernel(*refs, r):
    if r == 0:
        a_ref, o_ref = refs
        o_ref[...] = a_ref[0].astype(o_ref.dtype)
    else:
        a_ref, b_ref, o_ref = refs
        o_ref[...] = jnp.concatenate([a_ref[0, r:, :], b_ref[0, :r, :]], axis=0).astype(o_ref.dtype)


def window_cast(wt3, layer, off, width, name):
    _, _, d = wt3.shape
    tr = _pick(width, 256)
    start = (off // tr) * tr
    r = off - start
    if r % 8:
        return wt3[layer, off:off + width, :].astype(BF16)
    in_specs = [pl.BlockSpec((1, tr, d), lambda j: (layer, start // tr + j, 0))]
    if r:
        rb = LANES if r <= LANES else tr
        in_specs.append(pl.BlockSpec((1, rb, d), lambda j: (layer, (start + (j + 1) * tr) // rb, 0)))
    return pl.pallas_call(
        functools.partial(_window_cast_kernel, r=r),
        out_shape=jax.ShapeDtypeStruct((width, d), BF16),
        grid=(width // tr,),
        in_specs=in_specs,
        out_specs=pl.BlockSpec((tr, d), lambda j: (j, 0)),
        compiler_params=_cparams(("parallel",)),
        name=name,
    )(*([wt3] * len(in_specs)))


def _small_kernel(h_ref, w_ref, g_ref, bf_ref, cos_ref, sin_ref,
                  kvn_ref, kr_ref, lf_ref, *, kvl):
    acc = _dot_nt(h_ref[...], w_ref[...])
    kvn_ref[...] = _rms(acc[:, :kvl], g_ref[...]).astype(kvn_ref.dtype)
    r = acc[:, kvl:kvl + LANES]
    p = acc[:, kvl + LANES:kvl + 2 * LANES]
    kr_ref[...] = (r * cos_ref[...] + p * sin_ref[...]).astype(kr_ref.dtype)
    lf_ref[...] = _log_sigmoid(acc[:, kvl + 2 * LANES:kvl + 3 * LANES] + bf_ref[...])


def inproj_small(h, w, kv_norm, b_forget_pad, cos_t, sin_t, seq, name):
    m, d = h.shape
    kvl = kv_norm.shape[0]
    n = w.shape[0]
    tm = _pick(seq, 512)
    nseq = seq // tm
    return pl.pallas_call(
        functools.partial(_small_kernel, kvl=kvl),
        out_shape=(jax.ShapeDtypeStruct((m, kvl), BF16),
                   jax.ShapeDtypeStruct((m, LANES), BF16),
                   jax.ShapeDtypeStruct((m, LANES), F32)),
        grid=(m // tm,),
        in_specs=[pl.BlockSpec((tm, d), lambda i: (i, 0)),
                  pl.BlockSpec((n, d), lambda i: (0, 0)),
                  pl.BlockSpec((1, kvl), lambda i: (0, 0)),
                  pl.BlockSpec((1, LANES), lambda i: (0, 0)),
                  pl.BlockSpec((tm, LANES), lambda i: (i % nseq, 0)),
                  pl.BlockSpec((tm, LANES), lambda i: (i % nseq, 0))],
        out_specs=(pl.BlockSpec((tm, kvl), lambda i: (i, 0)),
                   pl.BlockSpec((tm, LANES), lambda i: (i, 0)),
                   pl.BlockSpec((tm, LANES), lambda i: (i, 0))),
        compiler_params=_cparams(("parallel",)),
        name=name,
    )(h, w, kv_norm.reshape(1, kvl), b_forget_pad, cos_t, sin_t)


def _mlaq_kernel(x_ref, wt_ref, cos_ref, sin_ref, o_ref, *, heads, qscale):
    hw = heads * HEAD_DIM
    x = x_ref[...]
    nope = _dot_nt(wt_ref[:hw, :], x)
    o_ref[0, :hw, :] = (nope * qscale).astype(o_ref.dtype)
    r = _dot_nt(wt_ref[hw:2 * hw, :], x)
    p = _dot_nt(wt_ref[2 * hw:, :], x)
    cos_t, sin_t = cos_ref[...] * qscale, sin_ref[...] * qscale
    for h in range(heads):
        sl = slice(h * HEAD_DIM, (h + 1) * HEAD_DIM)
        o_ref[0, hw + h * HEAD_DIM:hw + (h + 1) * HEAD_DIM, :] = (
            r[sl, :] * cos_t + p[sl, :] * sin_t).astype(o_ref.dtype)


def mla_q(qn, wt, cos_tt, sin_tt, seq, heads, qscale, t, name):
    m, ql = qn.shape
    n = wt.shape[0]
    nseq = seq // t
    return pl.pallas_call(
        functools.partial(_mlaq_kernel, heads=heads, qscale=qscale),
        out_shape=jax.ShapeDtypeStruct((m // t, 2 * heads * HEAD_DIM, t), BF16),
        grid=(m // t,),
        in_specs=[pl.BlockSpec((t, ql), lambda i: (i, 0)),
                  pl.BlockSpec((n, ql), lambda i: (0, 0)),
                  pl.BlockSpec((LANES, t), lambda i: (0, i % nseq)),
                  pl.BlockSpec((LANES, t), lambda i: (0, i % nseq))],
        out_specs=pl.BlockSpec((1, 2 * heads * HEAD_DIM, t), lambda i: (i, 0, 0)),
        compiler_params=_cparams(("parallel",)),
        name=name,
    )(qn, wt, cos_tt, sin_tt)


def _split(x, parts):
    out = []
    for _ in range(parts - 1):
        hi = x.astype(BF16)
        out.append(hi)
        x = x - hi.astype(F32)
    out.append(x.astype(BF16))
    return out


def _cumsum_kernel(x_ref, o_ref, *, seq, tc):
    rows = lax.broadcasted_iota(jnp.int32, (tc, tc), 0)
    cols = lax.broadcasted_iota(jnp.int32, (tc, tc), 1)
    tri = jnp.where(cols <= rows, 1.0, 0.0).astype(BF16)

    def body(c, carry):
        r0 = pl.multiple_of(c * tc, tc)
        x = x_ref[0, pl.ds(r0, tc), :]
        acc = jnp.zeros((tc, LANES), F32)
        for part in _split(x, 3):
            acc = acc + jnp.dot(tri, part, preferred_element_type=F32)
        acc = acc + carry
        o_ref[0, pl.ds(r0, tc), :] = acc * LOG2E
        return acc[tc - 1:tc, :]

    lax.fori_loop(0, seq // tc, body, jnp.zeros((1, LANES), F32))


def cumsum_rows(x, name):
    b, seq, n = x.shape
    tc = _pick(seq, 256)
    return pl.pallas_call(
        functools.partial(_cumsum_kernel, seq=seq, tc=tc),
        out_shape=jax.ShapeDtypeStruct((b, seq, n), F32),
        grid=(b,),
        in_specs=[pl.BlockSpec((1, seq, n), lambda i: (i, 0, 0))],
        out_specs=pl.BlockSpec((1, seq, n), lambda i: (i, 0, 0)),
        compiler_params=_cparams(("parallel",)),
        name=name,
    )(x)


def _head(ref, i, rows=slice(None)):
    return ref[rows, i * HEAD_DIM:(i + 1) * HEAD_DIM]


def _head_t(ref, i):
    return ref[0, i * HEAD_DIM:(i + 1) * HEAD_DIM, :]


def _pipeline(n, chunk_of, scores_into, consume, buf_a, buf_b, mask_last, prefilled=False):
    if not prefilled:
        scores_into(buf_a, chunk_of(0))

    def pair(p, _):
        scores_into(buf_b, chunk_of(2 * p + 1))
        consume(buf_a, chunk_of(2 * p), False)
        scores_into(buf_a, chunk_of(2 * p + 2))
        consume(buf_b, chunk_of(2 * p + 1), False)
        return 0

    lax.fori_loop(0, (n - 1) // 2, pair, 0)

    @pl.when(n % 2 == 1)
    def _():
        consume(buf_a, chunk_of(n - 1), mask_last)

    @pl.when(n % 2 == 0)
    def _():
        scores_into(buf_b, chunk_of(n - 1))
        consume(buf_a, chunk_of(n - 2), False)
        consume(buf_b, chunk_of(n - 1), mask_last)


def _softmax_consume(buf, vts, mask, m_ref, l_ref, acc_ref):
    g = len(vts)
    t = buf.shape[-1]
    ones = jnp.ones((SUM_ROWS, t), BF16)
    ps, alphas = [], []
    for i in range(g):
        s = buf[i] if mask is None else jnp.where(mask, buf[i], NEG_BIG)
        m_old = m_ref[i]
        m_new = jnp.maximum(m_old, jnp.max(s, axis=0, keepdims=True))
        alphas.append(jnp.exp2(m_old - m_new))
        ps.append(jnp.exp2(s - m_new).astype(BF16))
        m_ref[i] = m_new
    pvs = [jnp.dot(jnp.concatenate([vt, ones], axis=0), p, preferred_element_type=F32)
           for vt, p in zip(vts, ps)]
    for i in range(g):
        acc_ref[i] = alphas[i] * acc_ref[i] + pvs[i][:HEAD_DIM]
        l_ref[i] = alphas[i] * l_ref[i] + pvs[i][HEAD_DIM:HEAD_DIM + 1]


def _softmax_init(m_ref, l_ref, acc_ref):
    m_ref[...] = jnp.full(m_ref.shape, NEG_BIG, F32)
    l_ref[...] = jnp.zeros(l_ref.shape, F32)
    acc_ref[...] = jnp.zeros(acc_ref.shape, F32)


def _softmax_store(o_ref, l_ref, acc_ref):
    for i in range(acc_ref.shape[0]):
        o_ref[:, i * HEAD_DIM:(i + 1) * HEAD_DIM] = (
            acc_ref[i] * (1.0 / l_ref[i])).T.astype(o_ref.dtype)


def _attn_scratch(g, t):
    return [pltpu.VMEM((g, t, t), F32), pltpu.VMEM((g, t, t), F32),
            pltpu.VMEM((g, 1, t), F32), pltpu.VMEM((g, 1, t), F32),
            pltpu.VMEM((g, HEAD_DIM, t), F32)]


def _mla_kernel(qn_ref, qr_ref, kn_ref, kr_ref, vt_ref, o_ref,
                sa_ref, sb_ref, m_ref, l_ref, acc_ref, *, t, g):
    qi = pl.program_id(2)
    qts = [jnp.concatenate([_head_t(qn_ref, i), _head_t(qr_ref, i)], axis=0) for i in range(g)]
    keys = lax.broadcasted_iota(jnp.int32, (t, t), 0) // CHUNK
    queries = lax.broadcasted_iota(jnp.int32, (t, t), 1) // CHUNK
    mask = keys <= queries

    def scores_into(buf, j):
        rows = pl.ds(pl.multiple_of(j * t, t), t)
        kr = kr_ref[rows, :]
        for i in range(g):
            buf[i] = jnp.dot(jnp.concatenate([_head(kn_ref, i, rows), kr], axis=-1), qts[i],
                             preferred_element_type=F32)

    def consume(buf, j, masked):
        vts = [vt_ref[j, i * HEAD_DIM:(i + 1) * HEAD_DIM, :] for i in range(g)]
        _softmax_consume(buf, vts, mask if masked else None, m_ref, l_ref, acc_ref)

    _softmax_init(m_ref, l_ref, acc_ref)
    _pipeline(qi + 1, lambda n: n, scores_into, consume, sa_ref, sb_ref, True)
    _softmax_store(o_ref, l_ref, acc_ref)


def mla_attention(q, kn, kr, vt, batch, seq, heads, name):
    m = kn.shape[0]
    t = vt.shape[2]
    nq = seq // t
    g = _head_group(heads)
    gw, hb = g * HEAD_DIM, heads // g
    return pl.pallas_call(
        functools.partial(_mla_kernel, t=t, g=g),
        out_shape=jax.ShapeDtypeStruct((m, heads * HEAD_DIM), BF16),
        grid=(batch, hb, nq),
        in_specs=[pl.BlockSpec((1, gw, t), lambda b, h, i: (b * nq + i, h, 0)),
                  pl.BlockSpec((1, gw, t), lambda b, h, i: (b * nq + i, hb + h, 0)),
                  pl.BlockSpec((seq, gw), lambda b, h, i: (b, h)),
                  pl.BlockSpec((seq, LANES), lambda b, h, i: (b, 0)),
                  pl.BlockSpec((nq, gw, t), lambda b, h, i: (b, h, 0))],
        out_specs=pl.BlockSpec((t, gw), lambda b, h, i: (b * nq + i, h)),
        scratch_shapes=_attn_scratch(g, t),
        compiler_params=_cparams(("parallel", "parallel", "arbitrary")),
        name=name,
    )(q, q, kn, kr, vt)


def _fox_kernel(qt_ref, k_ref, vt_ref, cq_ref, ck_ref, o_ref,
                sa_ref, sb_ref, m_ref, l_ref, acc_ref, *, t, g):
    qi = pl.program_id(2)
    qts = [_head_t(qt_ref, i) for i in range(g)]
    cqs = [cq_ref[i] for i in range(g)]
    keys = lax.broadcasted_iota(jnp.int32, (t, t), 0)
    queries = lax.broadcasted_iota(jnp.int32, (t, t), 1)
    mask = keys <= queries

    def scores_into(buf, j):
        rows = pl.ds(pl.multiple_of(j * t, t), t)
        for i in range(g):
            ck = ck_ref[i, rows, :]
            ck = jnp.concatenate([ck] * (t // LANES), axis=1)
            buf[i] = jnp.dot(_head(k_ref, i, rows), qts[i],
                             preferred_element_type=F32) + cqs[i] - ck

    def consume(buf, j, masked):
        vts = [vt_ref[j, i * HEAD_DIM:(i + 1) * HEAD_DIM, :] for i in range(g)]
        _softmax_consume(buf, vts, mask if masked else None, m_ref, l_ref, acc_ref)

    _softmax_init(m_ref, l_ref, acc_ref)
    _pipeline(qi + 1, lambda n: n, scores_into, consume, sa_ref, sb_ref, True)
    _softmax_store(o_ref, l_ref, acc_ref)


def fox_attention(qt, k, vt, c_row, c_rep, batch, seq, heads, name):
    m = k.shape[0]
    t = vt.shape[2]
    nq = seq // t
    g = _head_group(heads)
    gw, hb = g * HEAD_DIM, heads // g
    return pl.pallas_call(
        functools.partial(_fox_kernel, t=t, g=g),
        out_shape=jax.ShapeDtypeStruct((m, heads * HEAD_DIM), BF16),
        grid=(batch, hb, nq),
        in_specs=[pl.BlockSpec((1, gw, t), lambda b, h, i: (b * nq + i, h, 0)),
                  pl.BlockSpec((seq, gw), lambda b, h, i: (b, h)),
                  pl.BlockSpec((nq, gw, t), lambda b, h, i: (b, h, 0)),
                  pl.BlockSpec((g, 1, t), lambda b, h, i: (b * hb + h, 0, i)),
                  pl.BlockSpec((g, seq, LANES), lambda b, h, i: (b * hb + h, 0, 0))],
        out_specs=pl.BlockSpec((t, gw), lambda b, h, i: (b * nq + i, h)),
        scratch_shapes=_attn_scratch(g, t),
        compiler_params=_cparams(("parallel", "parallel", "arbitrary")),
        name=name,
    )(qt, k, vt, c_row, c_rep)


def _sb_kernel(qt_ref, k_ref, vt_ref, o_ref, za_ref, zb_ref, rest_ref, acc_ref, *, t, g):
    qi = pl.program_id(2)
    qts = [_head_t(qt_ref, i) for i in range(g)]
    keys = lax.broadcasted_iota(jnp.int32, (t, t), 0)
    queries = lax.broadcasted_iota(jnp.int32, (t, t), 1)
    upper = jnp.where(queries >= keys, 1.0, 0.0).astype(BF16)
    strict = keys < queries

    def scores_into(buf, j):
        rows = pl.ds(pl.multiple_of(j * t, t), t)
        for i in range(g):
            buf[i] = jnp.dot(_head(k_ref, i, rows), qts[i],
                             preferred_element_type=F32)

    def consume(buf, j, masked):
        loms = []
        for i in range(g):
            z = buf[i]
            nz = -z
            lom = jnp.minimum(nz, 0.0) - jnp.log(1.0 + jnp.exp2(jnp.minimum(z, nz))) * LOG2E
            loms.append(jnp.where(strict, lom, 0.0) if masked else lom)
        incs = [sum(jnp.dot(upper, part, preferred_element_type=F32) for part in _split(lom, 2))
                for lom in loms]
        avs = []
        for i in range(g):
            a = jnp.exp2(buf[i] + incs[i] + rest_ref[i])
            avs.append((jnp.where(strict, a, 0.0) if masked else a).astype(BF16))
            rest_ref[i] = rest_ref[i] + incs[i][0:1, :]
        pvs = [jnp.dot(vt_ref[j, i * HEAD_DIM:(i + 1) * HEAD_DIM, :], avs[i],
                       preferred_element_type=F32) for i in range(g)]
        for i in range(g):
            acc_ref[i] = acc_ref[i] + pvs[i]

    rest_ref[...] = jnp.zeros(rest_ref.shape, F32)
    acc_ref[...] = jnp.zeros(acc_ref.shape, F32)
    scores_into(za_ref, qi)

    @pl.when(qi == 0)
    def _():
        consume(za_ref, qi, True)

    @pl.when(qi > 0)
    def _():
        scores_into(zb_ref, qi - 1)
        consume(za_ref, qi, True)
        _pipeline(qi, lambda n: qi - 1 - n, scores_into, consume, zb_ref, za_ref, False,
                  prefilled=True)

    for i in range(g):
        o_ref[:, i * HEAD_DIM:(i + 1) * HEAD_DIM] = acc_ref[i].T.astype(o_ref.dtype)


def sb_attention(qt, k, vt, batch, seq, heads, name):
    m = k.shape[0]
    t = vt.shape[2]
    nq = seq // t
    g = _head_group(heads)
    gw, hb = g * HEAD_DIM, heads // g
    return pl.pallas_call(
        functools.partial(_sb_kernel, t=t, g=g),
        out_shape=jax.ShapeDtypeStruct((m, heads * HEAD_DIM), BF16),
        grid=(batch, hb, nq),
        in_specs=[pl.BlockSpec((1, gw, t), lambda b, h, i: (b * nq + i, h, 0)),
                  pl.BlockSpec((seq, gw), lambda b, h, i: (b, h)),
                  pl.BlockSpec((nq, gw, t), lambda b, h, i: (b, h, 0))],
        out_specs=pl.BlockSpec((t, gw), lambda b, h, i: (b * nq + i, h)),
        scratch_shapes=[pltpu.VMEM((g, t, t), F32), pltpu.VMEM((g, t, t), F32),
                        pltpu.VMEM((g, 1, t), F32), pltpu.VMEM((g, HEAD_DIM, t), F32)],
        compiler_params=_cparams(("parallel", "parallel", "arbitrary")),
        name=name,
    )(qt, k, vt)


def _merge_kernel(oa_ref, ob_ref, oc_ref, wa_ref, wb_ref, wc_ref,
                  ga_ref, gb_ref, gc_ref, o_ref, wa_s, wb_s, wc_s):
    @pl.when(pl.program_id(1) == 0)
    def _():
        wa_s[...] = _wblock(wa_ref).astype(BF16)
        wb_s[...] = _wblock(wb_ref).astype(BF16)
        wc_s[...] = _wblock(wc_ref).astype(BF16)

    acc = ga_ref[...].astype(F32) * jnp.dot(oa_ref[...], wa_s[...], preferred_element_type=F32)
    acc += gb_ref[...].astype(F32) * jnp.dot(ob_ref[...], wb_s[...], preferred_element_type=F32)
    acc += gc_ref[...].astype(F32) * jnp.dot(oc_ref[...], wc_s[...], preferred_element_type=F32)
    o_ref[...] = acc.astype(o_ref.dtype)


def gated_merge(oa, ob, oc, wa, wb, wc, gates, layer, name):
    m = oa.shape[0]
    d = wa.shape[-1]
    tm, tn = _pick(m, 1024), _pick(d, 512)
    nj = d // tn
    row = lambda a: pl.BlockSpec((tm, a.shape[1]), lambda j, i: (i, 0))
    col = lambda a: _wspec(a, layer, (a.shape[-2], tn), lambda j, i: (0, j))
    gate = lambda g: pl.BlockSpec((tm, tn), lambda j, i: (i, g * nj + j))
    return pl.pallas_call(
        _merge_kernel,
        out_shape=jax.ShapeDtypeStruct((m, d), BF16),
        grid=(nj, m // tm),
        in_specs=[row(oa), row(ob), row(oc), col(wa), col(wb), col(wc),
                  gate(0), gate(1), gate(2)],
        out_specs=pl.BlockSpec((tm, tn), lambda j, i: (i, j)),
        scratch_shapes=[pltpu.VMEM((a.shape[-2], tn), BF16) for a in (wa, wb, wc)],
        compiler_params=_cparams(("parallel", "arbitrary")),
        name=name,
    )(oa, ob, oc, wa, wb, wc, gates, gates, gates)


def _ffn_up_kernel(h_ref, wg_ref, wu_ref, cw_ref, cb_ref, o_ref, carry_ref, wg_s, wu_s,
                   *, tiles_per_seq):
    i = pl.program_id(1)

    @pl.when(i == 0)
    def _():
        wg_s[...] = _wblock(wg_ref).astype(BF16)
        wu_s[...] = _wblock(wu_ref).astype(BF16)

    @pl.when(i % tiles_per_seq == 0)
    def _():
        carry_ref[...] = jnp.zeros_like(carry_ref)

    prev = carry_ref[...]
    g = jnp.dot(h_ref[...], wg_s[...], preferred_element_type=F32)
    u = jnp.dot(h_ref[...], wu_s[...], preferred_element_type=F32)
    tm = g.shape[0]
    carry_ref[...] = g[tm - 8:, :]
    row = lax.broadcasted_iota(jnp.int32, g.shape, 0)
    g1 = jnp.where(row == 0, prev[7:8, :], pltpu.roll(g, 1, 0))
    g2 = jnp.where(row == 0, prev[6:7, :],
                   jnp.where(row == 1, prev[7:8, :], pltpu.roll(g, 2, 0)))
    cw = cw_ref[...]
    conv = cb_ref[...] + cw[0:1, :] * g2 + cw[1:2, :] * g1 + cw[2:3, :] * g
    act = conv * (1.0 / (1.0 + jnp.exp(-conv)))
    o_ref[...] = (act * u).astype(o_ref.dtype)


def ffn_up(h, wg, wu, conv_w, conv_b, seq, layer, name):
    m, d = h.shape
    f = wg.shape[-1]
    tm, tn = _pick(seq, 512), _pick(f, 512)
    return pl.pallas_call(
        functools.partial(_ffn_up_kernel, tiles_per_seq=seq // tm),
        out_shape=jax.ShapeDtypeStruct((m, f), BF16),
        grid=(f // tn, m // tm),
        in_specs=[pl.BlockSpec((tm, d), lambda j, i: (i, 0)),
                  _wspec(wg, layer, (d, tn), lambda j, i: (0, j)),
                  _wspec(wu, layer, (d, tn), lambda j, i: (0, j)),
                  pl.BlockSpec((CONV_WIDTH, tn), lambda j, i: (0, j)),
                  pl.BlockSpec((1, tn), lambda j, i: (0, j))],
        out_specs=pl.BlockSpec((tm, tn), lambda j, i: (i, j)),
        scratch_shapes=[pltpu.VMEM((8, tn), F32), pltpu.VMEM((d, tn), BF16),
                        pltpu.VMEM((d, tn), BF16)],
        compiler_params=_cparams(("arbitrary", "arbitrary")),
        name=name,
    )(h, wg, wu, conv_w, conv_b.reshape(1, f))


def _rope_tables(seq):
    inv = 1.0 / (ROPE_THETA ** (jnp.arange(0, ROPE, 2, dtype=F32) / ROPE))
    ang = jnp.arange(seq, dtype=F32)[:, None] * inv[None, :]
    zeros = jnp.zeros((seq, LANES - ROPE), F32)
    cos_t = jnp.concatenate([jnp.cos(ang), jnp.cos(ang), zeros], axis=-1)
    sin_t = jnp.concatenate([jnp.sin(ang), jnp.sin(ang), zeros], axis=-1)
    return cos_t, sin_t


def _rope_partner(w):
    half = ROPE // 2
    return jnp.concatenate([-w[..., half:], w[..., :half]], axis=-1)


def _in_offsets(d):
    ql, kvl = d // 4, d // 8
    hf = d // (4 * HEAD_DIM)
    fw = hf * HEAD_DIM
    o_kv = ql
    o_kr = o_kv + kvl
    o_f = o_kr + ROPE
    o_fpre = o_f + 3 * fw
    o_s = o_fpre + hf
    o_g = o_s + 3 * fw
    return dict(kv=o_kv, kr=o_kr, f=o_f, fpre=o_fpre, s=o_s, g=o_g)


def _small_weight(win_kv, win_kr, win_fp, hf):
    d = win_kv.shape[1]
    zpad = jnp.zeros((LANES - ROPE, d), BF16)
    kr = win_kr[:ROPE]
    half = ROPE // 2
    partner = jnp.concatenate([-kr[half:], kr[:half]], axis=0)
    return jnp.concatenate([win_kv, kr, zpad, partner, zpad,
                            win_fp[:hf], jnp.zeros((LANES - hf, d), BF16)], axis=0)


def _layer_weights(d, w_uq, w_ukv):
    ql, kvl = d // 4, d // 8
    hm = d // (2 * HEAD_DIM)
    uq = w_uq.reshape(ql, hm, HEAD_DIM + ROPE)
    r = uq[:, :, HEAD_DIM:]
    z = jnp.zeros((ql, hm, LANES - ROPE), F32)
    w_q = jnp.concatenate(
        [uq[:, :, :HEAD_DIM].reshape(ql, hm * HEAD_DIM),
         jnp.concatenate([r, z], axis=-1).reshape(ql, hm * HEAD_DIM),
         jnp.concatenate([_rope_partner(r), z], axis=-1).reshape(ql, hm * HEAD_DIM)],
        axis=-1).T.astype(BF16)
    ukv = w_ukv.reshape(kvl, hm, 2 * HEAD_DIM)
    w_kv = jnp.concatenate([ukv[:, :, :HEAD_DIM].reshape(kvl, hm * HEAD_DIM),
                            ukv[:, :, HEAD_DIM:].reshape(kvl, hm * HEAD_DIM)], axis=-1).astype(BF16)
    return dict(uq=w_q, ukv=w_kv)


def kernel(x, attn_norm, w_in, b_forget, b_gate, q_norm, w_uq, kv_norm, w_ukv, w_br_mla, w_br_fox, w_br_sb, w_o, ffn_norm, w_ffn_gate, conv_w, conv_b, w_ffn_up, w_ffn_down, final_norm):
    batch, seq, d = x.shape
    depth = w_in.shape[0]
    m = batch * seq
    hm, hf = d // (2 * HEAD_DIM), d // (4 * HEAD_DIM)
    hs = hf
    fw = hf * HEAD_DIM
    t_att = _pick(seq, ATT_TILE)
    cos_t, sin_t = _rope_tables(seq)
    cos_tt, sin_tt = cos_t.T, sin_t.T
    xf = x.reshape(m, d)
    q_scale = LOG2E / math.sqrt(HEAD_DIM)
    mla_scale = LOG2E / math.sqrt(HEAD_DIM + ROPE)
    off = _in_offsets(d)

    w_in_t = jnp.swapaxes(w_in, 1, 2)
    w_down = w_ffn_down.astype(BF16)

    for l in range(depth):
        w = _layer_weights(d, w_uq[l], w_ukv[l])
        w_small = _small_weight(window_cast(w_in_t, l, off["kv"], d // 8, f"w_kvlat_{l}"),
                                window_cast(w_in_t, l, off["kr"], LANES, f"w_krope_{l}"),
                                window_cast(w_in_t, l, off["fpre"], LANES, f"w_fpre_{l}"), hf)
        w_ql = window_cast(w_in_t, l, 0, d // 4, f"w_qlat_{l}")
        w_f = window_cast(w_in_t, l, off["f"], 3 * fw, f"w_fox_{l}")
        w_s = window_cast(w_in_t, l, off["s"], 3 * fw, f"w_sb_{l}")
        w_g = window_cast(w_in_t, l, off["g"], 3 * d, f"w_gate_{l}")
        h = rmsnorm(xf, attn_norm[l], BF16, f"attn_norm_{l}")

        qn = matmul(h, w_ql, out_dtype=BF16, name=f"inproj_qlat_{l}", wt=True,
                    epilogue=_ep_rmsnorm, extras=[(q_norm[l].reshape(1, -1), "row")],
                    tn=w_ql.shape[0])
        bf_pad = jnp.zeros((1, LANES), F32).at[0, :hf].set(b_forget[l])
        kvn, kr, logf = inproj_small(h, w_small, kv_norm[l], bf_pad, cos_t, sin_t, seq,
                                     f"inproj_small_{l}")
        qt_f = matmul_tout(h, w_f, t_att, f"inproj_fqt_{l}", n_out=fw, wt=True, scale=q_scale)
        k_f = matmul(h, w_f, out_dtype=BF16, name=f"inproj_fk_{l}", col0=fw, n_out=fw, wt=True,
                     tn=1024)
        vt_f = matmul_tout(h, w_f, t_att, f"inproj_fvt_{l}", col0=2 * fw, wt=True)
        qt_s = matmul_tout(h, w_s, t_att, f"inproj_sqt_{l}", n_out=fw, wt=True, scale=q_scale)
        k_s = matmul(h, w_s, out_dtype=BF16, name=f"inproj_sk_{l}", col0=fw, n_out=fw, wt=True,
                     tn=1024)
        vt_s = matmul_tout(h, w_s, t_att, f"inproj_svt_{l}", col0=2 * fw, wt=True)
        gates = matmul(h, w_g, out_dtype=BF16, name=f"inproj_gate_{l}", wt=True,
                       epilogue=_ep_sigmoid_bias, extras=[(b_gate[l].reshape(1, -1), "row")],
                       tn=1024)

        q = mla_q(qn, w["uq"], cos_tt, sin_tt, seq, hm, mla_scale, t_att, f"mla_q_{l}")
        kn = matmul(kvn, w["ukv"], out_dtype=BF16, name=f"mla_k_{l}", n_out=hm * HEAD_DIM)
        vt_mla = matmul_tout(kvn, w["ukv"], t_att, f"mla_vt_{l}", col0=hm * HEAD_DIM)
        o_mla = mla_attention(q, kn, kr, vt_mla, batch, seq, hm, f"mla_attn_{l}")

        c = cumsum_rows(logf.reshape(batch, seq, LANES), f"fox_cumsum_{l}")
        c_heads = jnp.transpose(c[:, :, :hf], (0, 2, 1)).reshape(batch * hf, seq)
        c_rep = jnp.broadcast_to(c_heads[:, :, None], (batch * hf, seq, LANES))
        o_fox = fox_attention(qt_f, k_f, vt_f, c_heads.reshape(batch * hf, 1, seq), c_rep,
                              batch, seq, hf, f"fox_attn_{l}")

        o_sb = sb_attention(qt_s, k_s, vt_s, batch, seq, hs, f"sb_attn_{l}")

        merged = gated_merge(o_mla, o_fox, o_sb, w_br_mla, w_br_fox, w_br_sb, gates, l,
                             f"merge_{l}")
        xf = matmul_ws(merged, w_o, out_dtype=F32, name=f"out_proj_{l}", layer=l,
                       epilogue=_ep_residual, extras=[(xf, "tile")])

        h = rmsnorm(xf, ffn_norm[l], BF16, f"ffn_norm_{l}")
        act = ffn_up(h, w_ffn_gate, w_ffn_up, conv_w[l], conv_b[l], seq, l, f"ffn_up_{l}")
        xf = matmul(act, w_down, out_dtype=F32, name=f"ffn_down_{l}", layer=l,
                    epilogue=_ep_residual, extras=[(xf, "tile")], tk=4096)

    return rmsnorm(xf, final_norm, F32, "final_norm").reshape(batch, seq, d)
```

```python
import functools
import math

import jax
import jax.numpy as jnp
from jax import lax
from jax.experimental import pallas as pl
from jax.experimental.pallas import tpu as pltpu

HEAD_DIM = 128
CHUNK = 64
ROPE = 64
ROPE_THETA = 10000.0
CONV_WIDTH = 3
EPS = 1e-6
LANES = 128
NEG_BIG = -1e30
LOG2E = math.log2(math.e)
VMEM_LIMIT = 56 * 1024 * 1024
ATT_TILE = 256
ATT_HEADS = 4
SUM_ROWS = 16
MLA_KEYS = 512

F32 = jnp.float32
BF16 = jnp.bfloat16


def _cparams(sem):
    return pltpu.CompilerParams(dimension_semantics=sem, vmem_limit_bytes=VMEM_LIMIT)


def _pick(n, pref):
    t = min(n, pref)
    while n % t:
        t //= 2
    return t


def _head_group(heads):
    g = ATT_HEADS
    while heads % g:
        g //= 2
    return g


def _rms(x, g):
    return x * lax.rsqrt(jnp.mean(x * x, axis=-1, keepdims=True) + EPS) * g


def _sigmoid(x):
    return 1.0 / (1.0 + jnp.exp(-x))


def _log_sigmoid(x):
    return jnp.minimum(x, 0.0) - jnp.log(1.0 + jnp.exp(-jnp.abs(x)))


def _rmsnorm_kernel(x_ref, g_ref, o_ref):
    o_ref[...] = _rms(x_ref[...], g_ref[...]).astype(o_ref.dtype)


def rmsnorm(x, g, out_dtype, name):
    m, d = x.shape
    tm = _pick(m, 256)
    return pl.pallas_call(
        _rmsnorm_kernel,
        out_shape=jax.ShapeDtypeStruct((m, d), out_dtype),
        grid=(m // tm,),
        in_specs=[pl.BlockSpec((tm, d), lambda i: (i, 0)),
                  pl.BlockSpec((1, d), lambda i: (0, 0))],
        out_specs=pl.BlockSpec((tm, d), lambda i: (i, 0)),
        compiler_params=_cparams(("parallel",)),
        name=name,
    )(x, g.reshape(1, d))


def _dot_nt(a, b):
    return lax.dot_general(a, b, (((1,), (1,)), ((), ())), preferred_element_type=F32)


def _wblock(ref):
    return ref[0] if len(ref.shape) == 3 else ref[...]


def _wspec(w, layer, shape, index):
    if layer is None:
        return pl.BlockSpec(shape, index)
    return pl.BlockSpec((1,) + shape, lambda *g: (layer,) + index(*g))


def _mm_kernel(*refs, nk, n_extra, epilogue, wt):
    x_ref, w_ref = refs[0], refs[1]
    extras = refs[2:2 + n_extra]
    o_ref = refs[2 + n_extra]
    if nk == 1:
        if wt:
            acc = _dot_nt(x_ref[...], _wblock(w_ref))
        else:
            acc = jnp.dot(x_ref[...], _wblock(w_ref), preferred_element_type=F32)
        o_ref[...] = epilogue(acc, *extras).astype(o_ref.dtype)
        return
    acc_ref = refs[3 + n_extra]
    k = pl.program_id(2)

    @pl.when(k == 0)
    def _():
        acc_ref[...] = jnp.dot(x_ref[...], _wblock(w_ref), preferred_element_type=F32)

    @pl.when((k > 0) & (k < nk - 1))
    def _():
        acc_ref[...] += jnp.dot(x_ref[...], _wblock(w_ref), preferred_element_type=F32)

    @pl.when(k == nk - 1)
    def _():
        acc = acc_ref[...] + jnp.dot(x_ref[...], _wblock(w_ref), preferred_element_type=F32)
        o_ref[...] = epilogue(acc, *extras).astype(o_ref.dtype)


def matmul(x, w, *, out_dtype, name, epilogue=None, extras=(), tm=1024, tn=512, tk=None,
           col0=0, n_out=None, wt=False, layer=None):
    m, kd = x.shape
    n = w.shape[-2 if wt else -1] - col0 if n_out is None else n_out
    assert not (wt and tk is not None)
    tm, tn = _pick(m, tm), _pick(math.gcd(n, col0), tn)
    cb = col0 // tn
    tk = kd if tk is None else _pick(kd, tk)
    nk = kd // tk
    if epilogue is None:
        epilogue = lambda acc: acc
    in_specs = [pl.BlockSpec((tm, tk), lambda i, j, k: (i, k)),
                _wspec(w, layer, (tn, tk), lambda i, j, k: (cb + j, k)) if wt else
                _wspec(w, layer, (tk, tn), lambda i, j, k: (k, cb + j))]
    args = [x, w]
    for arr, kind in extras:
        if kind == "row":
            in_specs.append(pl.BlockSpec((1, tn), lambda i, j, k: (0, j)))
        else:
            in_specs.append(pl.BlockSpec((tm, tn), lambda i, j, k: (i, j)))
        args.append(arr)
    scratch = [pltpu.VMEM((tm, tn), F32)] if nk > 1 else []
    return pl.pallas_call(
        functools.partial(_mm_kernel, nk=nk, n_extra=len(extras), epilogue=epilogue, wt=wt),
        out_shape=jax.ShapeDtypeStruct((m, n), out_dtype),
        grid=(m // tm, n // tn, nk),
        in_specs=in_specs,
        out_specs=pl.BlockSpec((tm, tn), lambda i, j, k: (i, j)),
        scratch_shapes=scratch,
        compiler_params=_cparams(("parallel", "parallel", "arbitrary")),
        name=name,
    )(*args)


def _mm_ws_kernel(*refs, n_extra, epilogue):
    x_ref, w_ref = refs[0], refs[1]
    extras = refs[2:2 + n_extra]
    o_ref, w_s = refs[2 + n_extra], refs[3 + n_extra]

    @pl.when(pl.program_id(1) == 0)
    def _():
        w_s[...] = _wblock(w_ref).astype(BF16)

    acc = jnp.dot(x_ref[...], w_s[...], preferred_element_type=F32)
    o_ref[...] = epilogue(acc, *extras).astype(o_ref.dtype)


def matmul_ws(x, w, *, out_dtype, name, epilogue=None, extras=(), tm=1024, tn=512, layer=None):
    m, kd = x.shape
    n = w.shape[-1]
    tm, tn = _pick(m, tm), _pick(n, tn)
    if epilogue is None:
        epilogue = lambda acc: acc
    in_specs = [pl.BlockSpec((tm, kd), lambda j, i: (i, 0)),
                _wspec(w, layer, (kd, tn), lambda j, i: (0, j))]
    args = [x, w]
    for arr, kind in extras:
        if kind == "row":
            in_specs.append(pl.BlockSpec((1, tn), lambda j, i: (0, j)))
        else:
            in_specs.append(pl.BlockSpec((tm, tn), lambda j, i: (i, j)))
        args.append(arr)
    return pl.pallas_call(
        functools.partial(_mm_ws_kernel, n_extra=len(extras), epilogue=epilogue),
        out_shape=jax.ShapeDtypeStruct((m, n), out_dtype),
        grid=(n // tn, m // tm),
        in_specs=in_specs,
        out_specs=pl.BlockSpec((tm, tn), lambda j, i: (i, j)),
        scratch_shapes=[pltpu.VMEM((kd, tn), BF16)],
        compiler_params=_cparams(("parallel", "arbitrary")),
        name=name,
    )(*args)


def _ep_rmsnorm(acc, g_ref):
    return _rms(acc, g_ref[...])


def _ep_sigmoid_bias(acc, b_ref):
    return _sigmoid(acc + b_ref[...])


def _ep_residual(acc, r_ref):
    return r_ref[...] + acc


def _mm_tout_kernel(x_ref, w_ref, o_ref, *, t, wt, scale):
    if wt:
        res = _dot_nt(w_ref[...], x_ref[...]) * scale
        for c in range(o_ref.shape[0]):
            o_ref[c] = res[:, c * t:(c + 1) * t].astype(o_ref.dtype)
        return
    res = jnp.dot(x_ref[...], w_ref[...], preferred_element_type=F32) * scale
    for c in range(o_ref.shape[0]):
        o_ref[c] = res[c * t:(c + 1) * t, :].T.astype(o_ref.dtype)


def matmul_tout(x, w, t, name, *, col0=0, n_out=None, wt=False, scale=1.0):
    m, kd = x.shape
    n = w.shape[0 if wt else 1] - col0 if n_out is None else n_out
    tm, tn = _pick(m, 1024), _pick(math.gcd(n, col0), 512)
    cb = col0 // tn
    return pl.pallas_call(
        functools.partial(_mm_tout_kernel, t=t, wt=wt, scale=scale),
        out_shape=jax.ShapeDtypeStruct((m // t, n, t), BF16),
        grid=(m // tm, n // tn),
        in_specs=[pl.BlockSpec((tm, kd), lambda i, j: (i, 0)),
                  pl.BlockSpec((tn, kd), lambda i, j: (cb + j, 0)) if wt else
                  pl.BlockSpec((kd, tn), lambda i, j: (0, cb + j))],
        out_specs=pl.BlockSpec((tm // t, tn, t), lambda i, j: (i, j, 0)),
        compiler_params=_cparams(("parallel", "parallel")),
        name=name,
    )(x, w)


def _window_cast_kernel(*refs, r):
    if r == 0:
        a_ref, o_ref = refs
        o_ref[...] = a_ref[0].astype(o_ref.dtype)
    else:
        a_ref, b_ref, o_ref = refs
        o_ref[...] = jnp.concatenate([a_ref[0, r:, :], b_ref[0, :r, :]], axis=0).astype(o_ref.dtype)


def window_cast(wt3, layer, off, width, name):
    _, _, d = wt3.shape
    tr = _pick(width, 256)
    start = (off // tr) * tr
    r = off - start
    if r % 8:
        return wt3[layer, off:off + width, :].astype(BF16)
    in_specs = [pl.BlockSpec((1, tr, d), lambda j: (layer, start // tr + j, 0))]
    if r:
        rb = LANES if r <= LANES else tr
        in_specs.append(pl.BlockSpec((1, rb, d), lambda j: (layer, (start + (j + 1) * tr) // rb, 0)))
    return pl.pallas_call(
        functools.partial(_window_cast_kernel, r=r),
        out_shape=jax.ShapeDtypeStruct((width, d), BF16),
        grid=(width // tr,),
        in_specs=in_specs,
        out_specs=pl.BlockSpec((tr, d), lambda j: (j, 0)),
        compiler_params=_cparams(("parallel",)),
        name=name,
    )(*([wt3] * len(in_specs)))


def _small_kernel(h_ref, w_ref, g_ref, bf_ref, cos_ref, sin_ref,
                  kvn_ref, kr_ref, lf_ref, *, kvl):
    acc = _dot_nt(h_ref[...], w_ref[...])
    kvn_ref[...] = _rms(acc[:, :kvl], g_ref[...]).astype(kvn_ref.dtype)
    r = acc[:, kvl:kvl + LANES]
    p = acc[:, kvl + LANES:kvl + 2 * LANES]
    kr_ref[...] = (r * cos_ref[...] + p * sin_ref[...]).astype(kr_ref.dtype)
    lf_ref[...] = _log_sigmoid(acc[:, kvl + 2 * LANES:kvl + 3 * LANES] + bf_ref[...])


def inproj_small(h, w, kv_norm, b_forget_pad, cos_t, sin_t, seq, name):
    m, d = h.shape
    kvl = kv_norm.shape[0]
    n = w.shape[0]
    tm = _pick(seq, 512)
    nseq = seq // tm
    return pl.pallas_call(
        functools.partial(_small_kernel, kvl=kvl),
        out_shape=(jax.ShapeDtypeStruct((m, kvl), BF16),
                   jax.ShapeDtypeStruct((m, LANES), BF16),
                   jax.ShapeDtypeStruct((m, LANES), F32)),
        grid=(m // tm,),
        in_specs=[pl.BlockSpec((tm, d), lambda i: (i, 0)),
                  pl.BlockSpec((n, d), lambda i: (0, 0)),
                  pl.BlockSpec((1, kvl), lambda i: (0, 0)),
                  pl.BlockSpec((1, LANES), lambda i: (0, 0)),
                  pl.BlockSpec((tm, LANES), lambda i: (i % nseq, 0)),
                  pl.BlockSpec((tm, LANES), lambda i: (i % nseq, 0))],
        out_specs=(pl.BlockSpec((tm, kvl), lambda i: (i, 0)),
                   pl.BlockSpec((tm, LANES), lambda i: (i, 0)),
                   pl.BlockSpec((tm, LANES), lambda i: (i, 0))),
        compiler_params=_cparams(("parallel",)),
        name=name,
    )(h, w, kv_norm.reshape(1, kvl), b_forget_pad, cos_t, sin_t)


def _mlaq_kernel(x_ref, wt_ref, cos_ref, sin_ref, o_ref, *, heads, qscale):
    hw = heads * HEAD_DIM
    x = x_ref[...]
    nope = _dot_nt(wt_ref[:hw, :], x)
    o_ref[0, :hw, :] = (nope * qscale).astype(o_ref.dtype)
    r = _dot_nt(wt_ref[hw:2 * hw, :], x)
    p = _dot_nt(wt_ref[2 * hw:, :], x)
    cos_t, sin_t = cos_ref[...] * qscale, sin_ref[...] * qscale
    for h in range(heads):
        sl = slice(h * HEAD_DIM, (h + 1) * HEAD_DIM)
        o_ref[0, hw + h * HEAD_DIM:hw + (h + 1) * HEAD_DIM, :] = (
            r[sl, :] * cos_t + p[sl, :] * sin_t).astype(o_ref.dtype)


def mla_q(qn, wt, cos_tt, sin_tt, seq, heads, qscale, t, name):
    m, ql = qn.shape
    n = wt.shape[0]
    nseq = seq // t
    return pl.pallas_call(
        functools.partial(_mlaq_kernel, heads=heads, qscale=qscale),
        out_shape=jax.ShapeDtypeStruct((m // t, 2 * heads * HEAD_DIM, t), BF16),
        grid=(m // t,),
        in_specs=[pl.BlockSpec((t, ql), lambda i: (i, 0)),
                  pl.BlockSpec((n, ql), lambda i: (0, 0)),
                  pl.BlockSpec((LANES, t), lambda i: (0, i % nseq)),
                  pl.BlockSpec((LANES, t), lambda i: (0, i % nseq))],
        out_specs=pl.BlockSpec((1, 2 * heads * HEAD_DIM, t), lambda i: (i, 0, 0)),
        compiler_params=_cparams(("parallel",)),
        name=name,
    )(qn, wt, cos_tt, sin_tt)


def _split(x, parts):
    out = []
    for _ in range(parts - 1):
        hi = x.astype(BF16)
        out.append(hi)
        x = x - hi.astype(F32)
    out.append(x.astype(BF16))
    return out


def _cumsum_kernel(x_ref, o_ref, *, seq, tc):
    rows = lax.broadcasted_iota(jnp.int32, (tc, tc), 0)
    cols = lax.broadcasted_iota(jnp.int32, (tc, tc), 1)
    tri = jnp.where(cols <= rows, 1.0, 0.0).astype(BF16)

    def body(c, carry):
        r0 = pl.multiple_of(c * tc, tc)
        x = x_ref[0, pl.ds(r0, tc), :]
        acc = jnp.zeros((tc, LANES), F32)
        for part in _split(x, 3):
            acc = acc + jnp.dot(tri, part, preferred_element_type=F32)
        acc = acc + carry
        o_ref[0, pl.ds(r0, tc), :] = acc * LOG2E
        return acc[tc - 1:tc, :]

    lax.fori_loop(0, seq // tc, body, jnp.zeros((1, LANES), F32))


def cumsum_rows(x, name):
    b, seq, n = x.shape
    tc = _pick(seq, 256)
    return pl.pallas_call(
        functools.partial(_cumsum_kernel, seq=seq, tc=tc),
        out_shape=jax.ShapeDtypeStruct((b, seq, n), F32),
        grid=(b,),
        in_specs=[pl.BlockSpec((1, seq, n), lambda i: (i, 0, 0))],
        out_specs=pl.BlockSpec((1, seq, n), lambda i: (i, 0, 0)),
        compiler_params=_cparams(("parallel",)),
        name=name,
    )(x)


def _head(ref, i, rows=slice(None)):
    return ref[rows, i * HEAD_DIM:(i + 1) * HEAD_DIM]


def _head_t(ref, i):
    return ref[0, i * HEAD_DIM:(i + 1) * HEAD_DIM, :]


def _pipeline(n, chunk_of, scores_into, consume, buf_a, buf_b, mask_last, prefilled=False):
    if not prefilled:
        scores_into(buf_a, chunk_of(0))

    def pair(p, _):
        scores_into(buf_b, chunk_of(2 * p + 1))
        consume(buf_a, chunk_of(2 * p), False)
        scores_into(buf_a, chunk_of(2 * p + 2))
        consume(buf_b, chunk_of(2 * p + 1), False)
        return 0

    lax.fori_loop(0, (n - 1) // 2, pair, 0)

    @pl.when(n % 2 == 1)
    def _():
        consume(buf_a, chunk_of(n - 1), mask_last)

    @pl.when(n % 2 == 0)
    def _():
        scores_into(buf_b, chunk_of(n - 1))
        consume(buf_a, chunk_of(n - 2), False)
        consume(buf_b, chunk_of(n - 1), mask_last)


def _softmax_consume(buf, vts, mask, m_ref, l_ref, acc_ref):
    g = len(vts)
    ones = jnp.ones((SUM_ROWS, vts[0].shape[-1]), BF16)
    ps, alphas = [], []
    for i in range(g):
        s = buf[i] if mask is None else jnp.where(mask, buf[i], NEG_BIG)
        m_old = m_ref[i]
        m_new = jnp.maximum(m_old, jnp.max(s, axis=0, keepdims=True))
        alphas.append(jnp.exp2(m_old - m_new))
        ps.append(jnp.exp2(s - m_new).astype(BF16))
        m_ref[i] = m_new
    pvs = [jnp.dot(jnp.concatenate([vt, ones], axis=0), p, preferred_element_type=F32)
           for vt, p in zip(vts, ps)]
    for i in range(g):
        acc_ref[i] = alphas[i] * acc_ref[i] + pvs[i][:HEAD_DIM]
        l_ref[i] = alphas[i] * l_ref[i] + pvs[i][HEAD_DIM:HEAD_DIM + 1]


def _softmax_init(m_ref, l_ref, acc_ref):
    m_ref[...] = jnp.full(m_ref.shape, NEG_BIG, F32)
    l_ref[...] = jnp.zeros(l_ref.shape, F32)
    acc_ref[...] = jnp.zeros(acc_ref.shape, F32)


def _softmax_store(o_ref, l_ref, acc_ref):
    for i in range(acc_ref.shape[0]):
        o_ref[:, i * HEAD_DIM:(i + 1) * HEAD_DIM] = (
            acc_ref[i] * (1.0 / l_ref[i])).T.astype(o_ref.dtype)


def _attn_scratch(g, t, tk=None):
    tk = t if tk is None else tk
    return [pltpu.VMEM((g, tk, t), F32), pltpu.VMEM((g, tk, t), F32),
            pltpu.VMEM((g, 1, t), F32), pltpu.VMEM((g, 1, t), F32),
            pltpu.VMEM((g, HEAD_DIM, t), F32)]


def _mla_kernel(qn_ref, qr_ref, kn_ref, kr_ref, vt_ref, o_ref,
                sa_ref, sb_ref, m_ref, l_ref, acc_ref, *, t, tk, g):
    qi = pl.program_id(2)
    qts = [jnp.concatenate([_head_t(qn_ref, i), _head_t(qr_ref, i)], axis=0) for i in range(g)]
    last = (qi * t) // tk
    keys = lax.broadcasted_iota(jnp.int32, (tk, t), 0)
    queries = lax.broadcasted_iota(jnp.int32, (tk, t), 1) + (qi * t - last * tk)
    mask = keys // CHUNK <= queries // CHUNK

    def scores_into(buf, j):
        rows = pl.ds(pl.multiple_of(j * tk, tk), tk)
        kr = kr_ref[rows, :]
        for i in range(g):
            buf[i] = jnp.dot(jnp.concatenate([_head(kn_ref, i, rows), kr], axis=-1), qts[i],
                             preferred_element_type=F32)

    def consume(buf, j, masked):
        vts = [vt_ref[j, i * HEAD_DIM:(i + 1) * HEAD_DIM, :] for i in range(g)]
        _softmax_consume(buf, vts, mask if masked else None, m_ref, l_ref, acc_ref)

    _softmax_init(m_ref, l_ref, acc_ref)
    _pipeline(last + 1, lambda n: n, scores_into, consume, sa_ref, sb_ref, True)
    _softmax_store(o_ref, l_ref, acc_ref)


def mla_attention(q, kn, kr, vt, batch, seq, heads, name):
    m = kn.shape[0]
    t, tk = q.shape[2], vt.shape[2]
    nq = seq // t
    g = _head_group(heads)
    gw, hb = g * HEAD_DIM, heads // g
    return pl.pallas_call(
        functools.partial(_mla_kernel, t=t, tk=tk, g=g),
        out_shape=jax.ShapeDtypeStruct((m, heads * HEAD_DIM), BF16),
        grid=(batch, hb, nq),
        in_specs=[pl.BlockSpec((1, gw, t), lambda b, h, i: (b * nq + i, h, 0)),
                  pl.BlockSpec((1, gw, t), lambda b, h, i: (b * nq + i, hb + h, 0)),
                  pl.BlockSpec((seq, gw), lambda b, h, i: (b, h)),
                  pl.BlockSpec((seq, LANES), lambda b, h, i: (b, 0)),
                  pl.BlockSpec((seq // tk, gw, tk), lambda b, h, i: (b, h, 0))],
        out_specs=pl.BlockSpec((t, gw), lambda b, h, i: (b * nq + i, h)),
        scratch_shapes=_attn_scratch(g, t, tk),
        compiler_params=_cparams(("parallel", "parallel", "arbitrary")),
        name=name,
    )(q, q, kn, kr, vt)


def _fox_kernel(qt_ref, k_ref, vt_ref, cq_ref, ck_ref, o_ref,
                sa_ref, sb_ref, m_ref, l_ref, acc_ref, *, t, g):
    qi = pl.program_id(2)
    qts = [_head_t(qt_ref, i) for i in range(g)]
    cqs = [cq_ref[i] for i in range(g)]
    keys = lax.broadcasted_iota(jnp.int32, (t, t), 0)
    queries = lax.broadcasted_iota(jnp.int32, (t, t), 1)
    mask = keys <= queries

    def scores_into(buf, j):
        rows = pl.ds(pl.multiple_of(j * t, t), t)
        for i in range(g):
            ck = ck_ref[i, rows, :]
            ck = jnp.concatenate([ck] * (t // LANES), axis=1)
            buf[i] = jnp.dot(_head(k_ref, i, rows), qts[i],
                             preferred_element_type=F32) + cqs[i] - ck

    def consume(buf, j, masked):
        vts = [vt_ref[j, i * HEAD_DIM:(i + 1) * HEAD_DIM, :] for i in range(g)]
        _softmax_consume(buf, vts, mask if masked else None, m_ref, l_ref, acc_ref)

    _softmax_init(m_ref, l_ref, acc_ref)
    _pipeline(qi + 1, lambda n: n, scores_into, consume, sa_ref, sb_ref, True)
    _softmax_store(o_ref, l_ref, acc_ref)


def fox_attention(qt, k, vt, c_row, c_rep, batch, seq, heads, name):
    m = k.shape[0]
    t = vt.shape[2]
    nq = seq // t
    g = _head_group(heads)
    gw, hb = g * HEAD_DIM, heads // g
    return pl.pallas_call(
        functools.partial(_fox_kernel, t=t, g=g),
        out_shape=jax.ShapeDtypeStruct((m, heads * HEAD_DIM), BF16),
        grid=(batch, hb, nq),
        in_specs=[pl.BlockSpec((1, gw, t), lambda b, h, i: (b * nq + i, h, 0)),
                  pl.BlockSpec((seq, gw), lambda b, h, i: (b, h)),
                  pl.BlockSpec((nq, gw, t), lambda b, h, i: (b, h, 0)),
                  pl.BlockSpec((g, 1, t), lambda b, h, i: (b * hb + h, 0, i)),
                  pl.BlockSpec((g, seq, LANES), lambda b, h, i: (b * hb + h, 0, 0))],
        out_specs=pl.BlockSpec((t, gw), lambda b, h, i: (b * nq + i, h)),
        scratch_shapes=_attn_scratch(g, t),
        compiler_params=_cparams(("parallel", "parallel", "arbitrary")),
        name=name,
    )(qt, k, vt, c_row, c_rep)


def _sb_kernel(qt_ref, k_ref, vt_ref, o_ref, za_ref, zb_ref, rest_ref, acc_ref, *, t, g):
    qi = pl.program_id(2)
    qts = [_head_t(qt_ref, i) for i in range(g)]
    keys = lax.broadcasted_iota(jnp.int32, (t, t), 0)
    queries = lax.broadcasted_iota(jnp.int32, (t, t), 1)
    upper = jnp.where(queries >= keys, 1.0, 0.0).astype(BF16)
    strict = keys < queries

    def scores_into(buf, j):
        rows = pl.ds(pl.multiple_of(j * t, t), t)
        for i in range(g):
            buf[i] = jnp.dot(_head(k_ref, i, rows), qts[i],
                             preferred_element_type=F32)

    def consume(buf, j, masked):
        loms = []
        for i in range(g):
            z = buf[i]
            nz = -z
            lom = jnp.minimum(nz, 0.0) - jnp.log(1.0 + jnp.exp2(jnp.minimum(z, nz))) * LOG2E
            loms.append(jnp.where(strict, lom, 0.0) if masked else lom)
        incs = [sum(jnp.dot(upper, part, preferred_element_type=F32) for part in _split(lom, 2))
                for lom in loms]
        avs = []
        for i in range(g):
            a = jnp.exp2(buf[i] + incs[i] + rest_ref[i])
            avs.append((jnp.where(strict, a, 0.0) if masked else a).astype(BF16))
            rest_ref[i] = rest_ref[i] + incs[i][0:1, :]
        pvs = [jnp.dot(vt_ref[j, i * HEAD_DIM:(i + 1) * HEAD_DIM, :], avs[i],
                       preferred_element_type=F32) for i in range(g)]
        for i in range(g):
            acc_ref[i] = acc_ref[i] + pvs[i]

    rest_ref[...] = jnp.zeros(rest_ref.shape, F32)
    acc_ref[...] = jnp.zeros(acc_ref.shape, F32)
    scores_into(za_ref, qi)

    @pl.when(qi == 0)
    def _():
        consume(za_ref, qi, True)

    @pl.when(qi > 0)
    def _():
        scores_into(zb_ref, qi - 1)
        consume(za_ref, qi, True)
        _pipeline(qi, lambda n: qi - 1 - n, scores_into, consume, zb_ref, za_ref, False,
                  prefilled=True)

    for i in range(g):
        o_ref[:, i * HEAD_DIM:(i + 1) * HEAD_DIM] = acc_ref[i].T.astype(o_ref.dtype)


def sb_attention(qt, k, vt, batch, seq, heads, name):
    m = k.shape[0]
    t = vt.shape[2]
    nq = seq // t
    g = _head_group(heads)
    gw, hb = g * HEAD_DIM, heads // g
    return pl.pallas_call(
        functools.partial(_sb_kernel, t=t, g=g),
        out_shape=jax.ShapeDtypeStruct((m, heads * HEAD_DIM), BF16),
        grid=(batch, hb, nq),
        in_specs=[pl.BlockSpec((1, gw, t), lambda b, h, i: (b * nq + i, h, 0)),
                  pl.BlockSpec((seq, gw), lambda b, h, i: (b, h)),
                  pl.BlockSpec((nq, gw, t), lambda b, h, i: (b, h, 0))],
        out_specs=pl.BlockSpec((t, gw), lambda b, h, i: (b * nq + i, h)),
        scratch_shapes=[pltpu.VMEM((g, t, t), F32), pltpu.VMEM((g, t, t), F32),
                        pltpu.VMEM((g, 1, t), F32), pltpu.VMEM((g, HEAD_DIM, t), F32)],
        compiler_params=_cparams(("parallel", "parallel", "arbitrary")),
        name=name,
    )(qt, k, vt)


def _merge_kernel(oa_ref, ob_ref, oc_ref, wa_ref, wb_ref, wc_ref,
                  ga_ref, gb_ref, gc_ref, o_ref, wa_s, wb_s, wc_s):
    @pl.when(pl.program_id(1) == 0)
    def _():
        wa_s[...] = _wblock(wa_ref).astype(BF16)
        wb_s[...] = _wblock(wb_ref).astype(BF16)
        wc_s[...] = _wblock(wc_ref).astype(BF16)

    acc = ga_ref[...].astype(F32) * jnp.dot(oa_ref[...], wa_s[...], preferred_element_type=F32)
    acc += gb_ref[...].astype(F32) * jnp.dot(ob_ref[...], wb_s[...], preferred_element_type=F32)
    acc += gc_ref[...].astype(F32) * jnp.dot(oc_ref[...], wc_s[...], preferred_element_type=F32)
    o_ref[...] = acc.astype(o_ref.dtype)


def gated_merge(oa, ob, oc, wa, wb, wc, gates, layer, name):
    m = oa.shape[0]
    d = wa.shape[-1]
    tm, tn = _pick(m, 1024), _pick(d, 512)
    nj = d // tn
    row = lambda a: pl.BlockSpec((tm, a.shape[1]), lambda j, i: (i, 0))
    col = lambda a: _wspec(a, layer, (a.shape[-2], tn), lambda j, i: (0, j))
    gate = lambda g: pl.BlockSpec((tm, tn), lambda j, i: (i, g * nj + j))
    return pl.pallas_call(
        _merge_kernel,
        out_shape=jax.ShapeDtypeStruct((m, d), BF16),
        grid=(nj, m // tm),
        in_specs=[row(oa), row(ob), row(oc), col(wa), col(wb), col(wc),
                  gate(0), gate(1), gate(2)],
        out_specs=pl.BlockSpec((tm, tn), lambda j, i: (i, j)),
        scratch_shapes=[pltpu.VMEM((a.shape[-2], tn), BF16) for a in (wa, wb, wc)],
        compiler_params=_cparams(("parallel", "arbitrary")),
        name=name,
    )(oa, ob, oc, wa, wb, wc, gates, gates, gates)


def _ffn_up_kernel(h_ref, wg_ref, wu_ref, cw_ref, cb_ref, o_ref, carry_ref, wg_s, wu_s,
                   *, tiles_per_seq):
    i = pl.program_id(1)

    @pl.when(i == 0)
    def _():
        wg_s[...] = _wblock(wg_ref).astype(BF16)
        wu_s[...] = _wblock(wu_ref).astype(BF16)

    @pl.when(i % tiles_per_seq == 0)
    def _():
        carry_ref[...] = jnp.zeros_like(carry_ref)

    prev = carry_ref[...]
    g = jnp.dot(h_ref[...], wg_s[...], preferred_element_type=F32)
    u = jnp.dot(h_ref[...], wu_s[...], preferred_element_type=F32)
    tm = g.shape[0]
    carry_ref[...] = g[tm - 8:, :]
    row = lax.broadcasted_iota(jnp.int32, g.shape, 0)
    g1 = jnp.where(row == 0, prev[7:8, :], pltpu.roll(g, 1, 0))
    g2 = jnp.where(row == 0, prev[6:7, :],
                   jnp.where(row == 1, prev[7:8, :], pltpu.roll(g, 2, 0)))
    cw = cw_ref[...]
    conv = cb_ref[...] + cw[0:1, :] * g2 + cw[1:2, :] * g1 + cw[2:3, :] * g
    act = conv * (1.0 / (1.0 + jnp.exp(-conv)))
    o_ref[...] = (act * u).astype(o_ref.dtype)


def ffn_up(h, wg, wu, conv_w, conv_b, seq, layer, name):
    m, d = h.shape
    f = wg.shape[-1]
    tm, tn = _pick(seq, 512), _pick(f, 512)
    return pl.pallas_call(
        functools.partial(_ffn_up_kernel, tiles_per_seq=seq // tm),
        out_shape=jax.ShapeDtypeStruct((m, f), BF16),
        grid=(f // tn, m // tm),
        in_specs=[pl.BlockSpec((tm, d), lambda j, i: (i, 0)),
                  _wspec(wg, layer, (d, tn), lambda j, i: (0, j)),
                  _wspec(wu, layer, (d, tn), lambda j, i: (0, j)),
                  pl.BlockSpec((CONV_WIDTH, tn), lambda j, i: (0, j)),
                  pl.BlockSpec((1, tn), lambda j, i: (0, j))],
        out_specs=pl.BlockSpec((tm, tn), lambda j, i: (i, j)),
        scratch_shapes=[pltpu.VMEM((8, tn), F32), pltpu.VMEM((d, tn), BF16),
                        pltpu.VMEM((d, tn), BF16)],
        compiler_params=_cparams(("arbitrary", "arbitrary")),
        name=name,
    )(h, wg, wu, conv_w, conv_b.reshape(1, f))


def _rope_tables(seq):
    inv = 1.0 / (ROPE_THETA ** (jnp.arange(0, ROPE, 2, dtype=F32) / ROPE))
    ang = jnp.arange(seq, dtype=F32)[:, None] * inv[None, :]
    zeros = jnp.zeros((seq, LANES - ROPE), F32)
    cos_t = jnp.concatenate([jnp.cos(ang), jnp.cos(ang), zeros], axis=-1)
    sin_t = jnp.concatenate([jnp.sin(ang), jnp.sin(ang), zeros], axis=-1)
    return cos_t, sin_t


def _rope_partner(w):
    half = ROPE // 2
    return jnp.concatenate([-w[..., half:], w[..., :half]], axis=-1)


def _in_offsets(d):
    ql, kvl = d // 4, d // 8
    hf = d // (4 * HEAD_DIM)
    fw = hf * HEAD_DIM
    o_kv = ql
    o_kr = o_kv + kvl
    o_f = o_kr + ROPE
    o_fpre = o_f + 3 * fw
    o_s = o_fpre + hf
    o_g = o_s + 3 * fw
    return dict(kv=o_kv, kr=o_kr, f=o_f, fpre=o_fpre, s=o_s, g=o_g)


def _small_weight(win_kv, win_kr, win_fp, hf):
    d = win_kv.shape[1]
    zpad = jnp.zeros((LANES - ROPE, d), BF16)
    kr = win_kr[:ROPE]
    half = ROPE // 2
    partner = jnp.concatenate([-kr[half:], kr[:half]], axis=0)
    return jnp.concatenate([win_kv, kr, zpad, partner, zpad,
                            win_fp[:hf], jnp.zeros((LANES - hf, d), BF16)], axis=0)


def _layer_weights(d, w_uq, w_ukv):
    ql, kvl = d // 4, d // 8
    hm = d // (2 * HEAD_DIM)
    uq = w_uq.reshape(ql, hm, HEAD_DIM + ROPE)
    r = uq[:, :, HEAD_DIM:]
    z = jnp.zeros((ql, hm, LANES - ROPE), F32)
    w_q = jnp.concatenate(
        [uq[:, :, :HEAD_DIM].reshape(ql, hm * HEAD_DIM),
         jnp.concatenate([r, z], axis=-1).reshape(ql, hm * HEAD_DIM),
         jnp.concatenate([_rope_partner(r), z], axis=-1).reshape(ql, hm * HEAD_DIM)],
        axis=-1).T.astype(BF16)
    ukv = w_ukv.reshape(kvl, hm, 2 * HEAD_DIM)
    w_kv = jnp.concatenate([ukv[:, :, :HEAD_DIM].reshape(kvl, hm * HEAD_DIM),
                            ukv[:, :, HEAD_DIM:].reshape(kvl, hm * HEAD_DIM)], axis=-1).astype(BF16)
    return dict(uq=w_q, ukv=w_kv)


def kernel(x, attn_norm, w_in, b_forget, b_gate, q_norm, w_uq, kv_norm, w_ukv, w_br_mla, w_br_fox, w_br_sb, w_o, ffn_norm, w_ffn_gate, conv_w, conv_b, w_ffn_up, w_ffn_down, final_norm):
    batch, seq, d = x.shape
    depth = w_in.shape[0]
    m = batch * seq
    hm, hf = d // (2 * HEAD_DIM), d // (4 * HEAD_DIM)
    hs = hf
    fw = hf * HEAD_DIM
    t_att = _pick(seq, ATT_TILE)
    cos_t, sin_t = _rope_tables(seq)
    cos_tt, sin_tt = cos_t.T, sin_t.T
    xf = x.reshape(m, d)
    q_scale = LOG2E / math.sqrt(HEAD_DIM)
    mla_scale = LOG2E / math.sqrt(HEAD_DIM + ROPE)
    off = _in_offsets(d)

    w_in_t = jnp.swapaxes(w_in, 1, 2)
    w_down = w_ffn_down.astype(BF16)

    for l in range(depth):
        w = _layer_weights(d, w_uq[l], w_ukv[l])
        w_small = _small_weight(window_cast(w_in_t, l, off["kv"], d // 8, f"w_kvlat_{l}"),
                                window_cast(w_in_t, l, off["kr"], LANES, f"w_krope_{l}"),
                                window_cast(w_in_t, l, off["fpre"], LANES, f"w_fpre_{l}"), hf)
        w_ql = window_cast(w_in_t, l, 0, d // 4, f"w_qlat_{l}")
        w_f = window_cast(w_in_t, l, off["f"], 3 * fw, f"w_fox_{l}")
        w_s = window_cast(w_in_t, l, off["s"], 3 * fw, f"w_sb_{l}")
        w_g = window_cast(w_in_t, l, off["g"], 3 * d, f"w_gate_{l}")
        h = rmsnorm(xf, attn_norm[l], BF16, f"attn_norm_{l}")

        qn = matmul(h, w_ql, out_dtype=BF16, name=f"inproj_qlat_{l}", wt=True,
                    epilogue=_ep_rmsnorm, extras=[(q_norm[l].reshape(1, -1), "row")],
                    tn=w_ql.shape[0])
        bf_pad = jnp.zeros((1, LANES), F32).at[0, :hf].set(b_forget[l])
        kvn, kr, logf = inproj_small(h, w_small, kv_norm[l], bf_pad, cos_t, sin_t, seq,
                                     f"inproj_small_{l}")
        qt_f = matmul_tout(h, w_f, t_att, f"inproj_fqt_{l}", n_out=fw, wt=True, scale=q_scale)
        k_f = matmul(h, w_f, out_dtype=BF16, name=f"inproj_fk_{l}", col0=fw, n_out=fw, wt=True,
                     tn=1024)
        vt_f = matmul_tout(h, w_f, t_att, f"inproj_fvt_{l}", col0=2 * fw, wt=True)
        qt_s = matmul_tout(h, w_s, t_att, f"inproj_sqt_{l}", n_out=fw, wt=True, scale=q_scale)
        k_s = matmul(h, w_s, out_dtype=BF16, name=f"inproj_sk_{l}", col0=fw, n_out=fw, wt=True,
                     tn=1024)
        vt_s = matmul_tout(h, w_s, t_att, f"inproj_svt_{l}", col0=2 * fw, wt=True)
        gates = matmul(h, w_g, out_dtype=BF16, name=f"inproj_gate_{l}", wt=True,
                       epilogue=_ep_sigmoid_bias, extras=[(b_gate[l].reshape(1, -1), "row")],
                       tn=1024)

        q = mla_q(qn, w["uq"], cos_tt, sin_tt, seq, hm, mla_scale, t_att, f"mla_q_{l}")
        kn = matmul(kvn, w["ukv"], out_dtype=BF16, name=f"mla_k_{l}", n_out=hm * HEAD_DIM)
        vt_mla = matmul_tout(kvn, w["ukv"], _pick(seq, MLA_KEYS), f"mla_vt_{l}", col0=hm * HEAD_DIM)
        o_mla = mla_attention(q, kn, kr, vt_mla, batch, seq, hm, f"mla_attn_{l}")

        c = cumsum_rows(logf.reshape(batch, seq, LANES), f"fox_cumsum_{l}")
        c_heads = jnp.transpose(c[:, :, :hf], (0, 2, 1)).reshape(batch * hf, seq)
        c_rep = jnp.broadcast_to(c_heads[:, :, None], (batch * hf, seq, LANES))
        o_fox = fox_attention(qt_f, k_f, vt_f, c_heads.reshape(batch * hf, 1, seq), c_rep,
                              batch, seq, hf, f"fox_attn_{l}")

        o_sb = sb_attention(qt_s, k_s, vt_s, batch, seq, hs, f"sb_attn_{l}")

        merged = gated_merge(o_mla, o_fox, o_sb, w_br_mla, w_br_fox, w_br_sb, gates, l,
                             f"merge_{l}")
        xf = matmul_ws(merged, w_o, out_dtype=F32, name=f"out_proj_{l}", layer=l,
                       epilogue=_ep_residual, extras=[(xf, "tile")])

        h = rmsnorm(xf, ffn_norm[l], BF16, f"ffn_norm_{l}")
        act = ffn_up(h, w_ffn_gate, w_ffn_up, conv_w[l], conv_b[l], seq, l, f"ffn_up_{l}")
        xf = matmul(act, w_down, out_dtype=F32, name=f"ffn_down_{l}", layer=l,
                    epilogue=_ep_residual, extras=[(xf, "tile")], tk=4096)

    return rmsnorm(xf, final_norm, F32, "final_norm").reshape(batch, seq, d)
```
